```python
import jax, jax.numpy as jnp
from jax import lax
import numpy as np

D_MODEL = 1024
BATCH = 8
SEQ = 4096
DEPTH = 1
DEC_BATCH = 128
DEC_SEQ = 1
PAST_LEN = 8192
PAGE_SIZE = 128

D_MIX = D_MODEL
D_ATTN = D_MIX // 2
D_RNN = D_MIX - D_ATTN
HEAD_DIM = 64
N_HEADS = D_ATTN // HEAD_DIM
N_KV_HEADS = N_HEADS
N_RNN_BLOCKS = 8
RNN_BLOCK_W = D_RNN // N_RNN_BLOCKS
CONV_W = 4
MOBA_BLOCK = 256
MOBA_TOPK = 3
Q_CHUNK = 128
ROPE_THETA = 500000.0
ROPE_DIM = HEAD_DIM // 4
LRU_C = 8.0
EPS = 1e-6
D_PROJ = 4 * D_ATTN + 2 * D_RNN
PROJ_SPLITS = (D_ATTN, 2 * D_ATTN, 3 * D_ATTN, 4 * D_ATTN, 4 * D_ATTN + D_RNN)

kernel_name = "hymba_moba_rglru_decode_step"


def rms_norm(x, g):
    xf = x.astype(jnp.float32)
    y = xf * lax.rsqrt(jnp.mean(xf * xf, axis=-1, keepdims=True) + EPS)
    return (y * g).astype(x.dtype)


def partial_rope(x, pos):
    half = ROPE_DIM // 2
    inv = ROPE_THETA ** (-jnp.arange(half, dtype=jnp.float32) * 2.0 / ROPE_DIM)
    ang = pos.astype(jnp.float32)[:, None] * inv
    cos = jnp.cos(ang)[:, None, :]
    sin = jnp.sin(ang)[:, None, :]
    x1 = x[..., :half].astype(jnp.float32)
    x2 = x[..., half:ROPE_DIM].astype(jnp.float32)
    rot = jnp.concatenate([x1 * cos - x2 * sin, x2 * cos + x1 * sin], axis=-1)
    return jnp.concatenate([rot.astype(x.dtype), x[..., ROPE_DIM:]], axis=-1)


def project_in(x, pos, rms_g, q_norm_g, k_norm_g, w_in):
    B, T, _ = x.shape
    h = rms_norm(x, rms_g)
    p = jnp.einsum('btd,de->bte', h, w_in)
    q, k, v, g_attn, x_rnn, g_rnn = jnp.split(p, PROJ_SPLITS, axis=-1)
    q = q.reshape(B, T, N_HEADS, HEAD_DIM)
    k = k.reshape(B, T, N_KV_HEADS, HEAD_DIM)
    v = v.reshape(B, T, N_KV_HEADS, HEAD_DIM)
    q = partial_rope(rms_norm(q, q_norm_g), pos)
    k = partial_rope(rms_norm(k, k_norm_g), pos)
    return q, k, v, g_attn, x_rnn, g_rnn


def combine_attention(q, ks, vs, sel_mask, ko, vo, own_mask):
    qf = q.astype(jnp.float32) * (HEAD_DIM ** -0.5)
    s_sel = jnp.einsum('...qhd,...qhjd->...qhj', qf, ks.astype(jnp.float32))
    s_own = jnp.einsum('...qhd,...khd->...qhk', qf, ko.astype(jnp.float32))
    s_sel = jnp.where(sel_mask, s_sel, -jnp.inf)
    s_own = jnp.where(own_mask, s_own, -jnp.inf)
    n_j = s_sel.shape[-1]
    p = jax.nn.softmax(jnp.concatenate([s_sel, s_own], axis=-1), axis=-1)
    out = (jnp.einsum('...qhj,...qhjd->...qhd', p[..., :n_j], vs.astype(jnp.float32))
           + jnp.einsum('...qhk,...khd->...qhd', p[..., n_j:], vo.astype(jnp.float32)))
    return out.astype(q.dtype)


def moba_prompt(q, k, v):
    B, S, H, hd = q.shape
    nb = -(-S // MOBA_BLOCK)
    pad = ((0, 0), (0, nb * MOBA_BLOCK - S), (0, 0), (0, 0))
    kp = jnp.pad(k, pad).reshape(B, nb, MOBA_BLOCK, H, hd)
    vp = jnp.pad(v, pad).reshape(B, nb, MOBA_BLOCK, H, hd)
    k_mean = jnp.mean(kp.astype(jnp.float32), axis=2)
    qpos = jnp.arange(S)
    s_blk = jnp.einsum('bshd,bnhd->bshn', q.astype(jnp.float32), k_mean)
    fully_past = jnp.arange(nb)[None, :] < (qpos // MOBA_BLOCK)[:, None]
    s_blk = jnp.where(fully_past[:, None, :], s_blk, -jnp.inf)
    n_sel = min(MOBA_TOPK, nb)
    _, top_idx = lax.top_k(s_blk, n_sel)
    nc = S // Q_CHUNK
    q_chunks = q.reshape(B * nc, Q_CHUNK, H, hd)
    i_chunks = top_idx.reshape(B * nc, Q_CHUNK, H, n_sel)
    b_ids = jnp.repeat(jnp.arange(B), nc)
    c_ids = jnp.tile(jnp.arange(nc), B)
    h_idx = jnp.arange(H)[None, :, None]

    def chunk_attend(args):
        qc, ic, b, c = args
        ks = kp[b, ic, :, h_idx].reshape(Q_CHUNK, H, n_sel * MOBA_BLOCK, hd)
        vs = vp[b, ic, :, h_idx].reshape(Q_CHUNK, H, n_sel * MOBA_BLOCK, hd)
        qpos_c = c * Q_CHUNK + jnp.arange(Q_CHUNK)
        rank_ok = jnp.arange(n_sel)[None, :] < (qpos_c // MOBA_BLOCK)[:, None]
        sel_mask = jnp.repeat(rank_ok, MOBA_BLOCK, axis=1)[:, None, :]
        ob = (c * Q_CHUNK) // MOBA_BLOCK
        ko = kp[b, ob]
        vo = vp[b, ob]
        kpos = ob * MOBA_BLOCK + jnp.arange(MOBA_BLOCK)
        own_mask = (kpos[None, :] <= qpos_c[:, None])[:, None, :]
        return combine_attention(qc, ks, vs, sel_mask, ko, vo, own_mask)

    out = lax.map(chunk_attend, (q_chunks, i_chunks, b_ids, c_ids))
    return out.reshape(B, S, H, hd)


def moba_sample(q, k_new, v_new, cache_k, cache_v, page_table, layer):
    DB, T, H, hd = q.shape
    ppb = MOBA_BLOCK // PAGE_SIZE
    own_blk = PAST_LEN // MOBA_BLOCK
    n_full = own_blk
    start = own_blk * MOBA_BLOCK
    n_sel = min(MOBA_TOPK, n_full)
    if n_sel > 0:
        pt_full = page_table[:, :n_full * ppb]
        page_means = jnp.mean(cache_k[layer, pt_full].astype(jnp.float32), axis=2)
        k_mean = jnp.mean(page_means.reshape(DB, n_full, ppb, H, hd), axis=2)
        s_blk = jnp.einsum('bthd,bnhd->bthn', q.astype(jnp.float32), k_mean)
        _, idx = lax.top_k(s_blk, n_sel)
        logical = idx[..., None] * ppb + jnp.arange(ppb)
        phys = page_table[jnp.arange(DB)[:, None, None, None, None], logical]
        h_idx = jnp.arange(H)[None, None, :, None, None]
        ks = cache_k[layer, phys, :, h_idx].reshape(DB, T, H, n_sel * MOBA_BLOCK, hd)
        vs = cache_v[layer, phys, :, h_idx].reshape(DB, T, H, n_sel * MOBA_BLOCK, hd)
    else:
        ks = jnp.zeros((DB, T, H, 0, hd), q.dtype)
        vs = jnp.zeros((DB, T, H, 0, hd), q.dtype)
    n_own_pages = (PAST_LEN - start) // PAGE_SIZE
    own_pages = page_table[:, start // PAGE_SIZE:PAST_LEN // PAGE_SIZE]
    ko_past = cache_k[layer, own_pages].reshape(DB, n_own_pages * PAGE_SIZE, H, hd)
    vo_past = cache_v[layer, own_pages].reshape(DB, n_own_pages * PAGE_SIZE, H, hd)
    ko = jnp.concatenate([ko_past.astype(k_new.dtype), k_new], axis=1)
    vo = jnp.concatenate([vo_past.astype(v_new.dtype), v_new], axis=1)
    n_past = n_own_pages * PAGE_SIZE
    j = jnp.arange(n_past + T)
    i = jnp.arange(T)
    own_mask = ((j[None, :] < n_past) | (j[None, :] - n_past <= i[:, None]))[:, None, :]
    return combine_attention(q, ks, vs, True, ko, vo, own_mask)


def causal_conv(x, buf, w, b):
    T = x.shape[1]
    xp = jnp.concatenate([buf.astype(x.dtype), x], axis=1)
    y = b + sum(w[j] * xp[:, j:j + T] for j in range(CONV_W))
    return y, xp[:, T:]


def rglru(x, h0, w_a, b_a, w_x, b_x, lam):
    B, T, D = x.shape
    xf = x.astype(jnp.float32)
    xb = xf.reshape(B, T, N_RNN_BLOCKS, RNN_BLOCK_W)
    r = jax.nn.sigmoid(jnp.einsum('btni,nij->btnj', xb, w_a).reshape(B, T, D) + b_a)
    gi = jax.nn.sigmoid(jnp.einsum('btni,nij->btnj', xb, w_x).reshape(B, T, D) + b_x)
    log_a = -LRU_C * r * jax.nn.softplus(-lam.astype(jnp.float32))
    a = jnp.exp(log_a)
    mult = jnp.sqrt(jnp.maximum(-jnp.expm1(2.0 * log_a), 0.0))
    bterm = mult * (gi * xf)
    bterm = bterm.at[:, 0].add(a[:, 0] * h0.astype(jnp.float32))

    def comb(left, right):
        a1, b1 = left
        a2, b2 = right
        return a1 * a2, a2 * b1 + b2

    _, h = lax.associative_scan(comb, (a, bterm), axis=1)
    return h.astype(x.dtype), h[:, -1].astype(x.dtype)


def rnn_branch(x_rnn, g_rnn, h0, buf, conv_w, conv_b, w_a, b_a, w_x, b_x, lam):
    xc, new_buf = causal_conv(x_rnn, buf, conv_w, conv_b)
    h, h_last = rglru(xc, h0, w_a, b_a, w_x, b_x, lam)
    return h * jax.nn.silu(g_rnn), h_last, new_buf


def project_out(x, attn, g_attn, rnn_out, w_out):
    B, T, _ = x.shape
    mixed = jnp.concatenate([attn.reshape(B, T, D_ATTN) * jax.nn.silu(g_attn), rnn_out], axis=-1)
    return x + jnp.einsum('bte,ed->btd', mixed, w_out)


def setup_inputs(seed: int = 0) -> dict:
    key = jax.random.key(seed)
    ks = jax.random.split(key, 20)
    f32 = jnp.float32
    n_pages = PAST_LEN // PAGE_SIZE
    n_pool = (DEC_BATCH * n_pages * 5) // 4
    x_prompt = jax.random.normal(ks[0], (BATCH, SEQ, D_MODEL), f32)
    x_sample = jax.random.normal(ks[1], (DEC_BATCH, DEC_SEQ, D_MODEL), f32)
    cache_k = jax.random.normal(ks[2], (DEPTH, n_pool, PAGE_SIZE, N_KV_HEADS, HEAD_DIM), f32)
    cache_v = jax.random.normal(ks[3], (DEPTH, n_pool, PAGE_SIZE, N_KV_HEADS, HEAD_DIM), f32)
    state_rglru_h = 0.5 * jax.random.normal(ks[4], (DEPTH, DEC_BATCH, D_RNN), f32)
    state_conv = jax.random.normal(ks[5], (DEPTH, DEC_BATCH, CONV_W - 1, D_RNN), f32)
    perm = jax.random.permutation(ks[6], n_pool)
    page_table = perm[:DEC_BATCH * n_pages].reshape(DEC_BATCH, n_pages).astype(jnp.int32)
    rms_g = 1.0 + 0.05 * jax.random.normal(ks[7], (DEPTH, D_MODEL), f32)
    q_norm_g = 1.0 + 0.05 * jax.random.normal(ks[8], (DEPTH, HEAD_DIM), f32)
    k_norm_g = 1.0 + 0.05 * jax.random.normal(ks[9], (DEPTH, HEAD_DIM), f32)
    w_in = jax.random.normal(ks[10], (DEPTH, D_MODEL, D_PROJ), f32) * D_MODEL ** -0.5
    conv_w = jax.random.normal(ks[11], (DEPTH, CONV_W, D_RNN), f32) * CONV_W ** -0.5
    conv_b = 0.01 * jax.random.normal(ks[12], (DEPTH, D_RNN), f32)
    w_a = jax.random.normal(ks[13], (DEPTH, N_RNN_BLOCKS, RNN_BLOCK_W, RNN_BLOCK_W), f32) * RNN_BLOCK_W ** -0.5
    b_a = 0.01 * jax.random.normal(ks[14], (DEPTH, D_RNN), f32)
    w_x = jax.random.normal(ks[15], (DEPTH, N_RNN_BLOCKS, RNN_BLOCK_W, RNN_BLOCK_W), f32) * RNN_BLOCK_W ** -0.5
    b_x = 0.01 * jax.random.normal(ks[16], (DEPTH, D_RNN), f32)
    u = jax.random.uniform(ks[17], (DEPTH, D_RNN), f32, 0.9, 0.999)
    s = u ** (1.0 / LRU_C)
    lru_lambda = jnp.log(s) - jnp.log1p(-s)
    w_out = jax.random.normal(ks[18], (DEPTH, D_MIX, D_MODEL), f32) * D_MIX ** -0.5
    return {"x_prompt": x_prompt, "x_sample": x_sample, "cache_k": cache_k, "cache_v": cache_v,
            "state_rglru_h": state_rglru_h, "state_conv": state_conv, "page_table": page_table,
            "rms_g": rms_g, "q_norm_g": q_norm_g, "k_norm_g": k_norm_g, "w_in": w_in,
            "conv_w": conv_w, "conv_b": conv_b, "w_a": w_a, "b_a": b_a, "w_x": w_x, "b_x": b_x,
            "lru_lambda": lru_lambda, "w_out": w_out}


def reference(x_prompt, x_sample, cache_k, cache_v, state_rglru_h, state_conv, page_table,
              rms_g, q_norm_g, k_norm_g, w_in, conv_w, conv_b, w_a, b_a, w_x, b_x,
              lru_lambda, w_out):
    pos_p = jnp.arange(x_prompt.shape[1])
    pos_s = PAST_LEN + jnp.arange(x_sample.shape[1])
    yp, ys = x_prompt, x_sample
    kp_l, vp_l, hp_l, cp_l, ks_l, vs_l, hs_l, cs_l = [], [], [], [], [], [], [], []
    for layer in range(DEPTH):
        q, k, v, g_attn, x_rnn, g_rnn = project_in(yp, pos_p, rms_g[layer], q_norm_g[layer],
                                                   k_norm_g[layer], w_in[layer])
        attn = moba_prompt(q, k, v)
        bp = yp.shape[0]
        rnn, h_last, buf = rnn_branch(x_rnn, g_rnn, jnp.zeros((bp, D_RNN), jnp.float32),
                                      jnp.zeros((bp, CONV_W - 1, D_RNN), yp.dtype),
                                      conv_w[layer], conv_b[layer], w_a[layer], b_a[layer],
                                      w_x[layer], b_x[layer], lru_lambda[layer])
        yp = project_out(yp, attn, g_attn, rnn, w_out[layer])
        kp_l.append(k); vp_l.append(v); hp_l.append(h_last); cp_l.append(buf)
        q, k, v, g_attn, x_rnn, g_rnn = project_in(ys, pos_s, rms_g[layer], q_norm_g[layer],
                                                   k_norm_g[layer], w_in[layer])
        attn = moba_sample(q, k, v, cache_k, cache_v, page_table, layer)
        rnn, h_last, buf = rnn_branch(x_rnn, g_rnn, state_rglru_h[layer], state_conv[layer],
                                      conv_w[layer], conv_b[layer], w_a[layer], b_a[layer],
                                      w_x[layer], b_x[layer], lru_lambda[layer])
        ys = project_out(ys, attn, g_attn, rnn, w_out[layer])
        ks_l.append(k); vs_l.append(v); hs_l.append(h_last); cs_l.append(buf)
    return (yp, ys, jnp.stack(kp_l), jnp.stack(vp_l), jnp.stack(hp_l), jnp.stack(cp_l),
            jnp.stack(ks_l), jnp.stack(vs_l), jnp.stack(hs_l), jnp.stack(cs_l))
```

```python
import functools

import jax
import jax.numpy as jnp
from jax import lax
from jax.experimental import pallas as pl
from jax.experimental.pallas import tpu as pltpu

ROPE_THETA = 500000.0
ROPE_FRACTION = 4
MOBA_BLOCK = 256
MOBA_TOPK = 3
LRU_C = 8.0
EPS = 1e-6
NEG = -1e30

LANES = 128
SUBLANES = 8
VMEM_LIMIT_BYTES = 56 * 1024 * 1024

F32 = jnp.float32
BF16 = jnp.bfloat16
NT_DIMS = (((1,), (1,)), ((), ()))


def _cparams(sem):
    return pltpu.CompilerParams(dimension_semantics=sem, vmem_limit_bytes=VMEM_LIMIT_BYTES)


def _silu(x):
    return x * jax.nn.sigmoid(x)


def _proj_in_kernel(x_ref, g_ref, w_ref, gq_ref, gk_ref, bd_ref, c_ref, s1_ref, s2_ref,
                    q_ref, k_ref, v_ref, ga_ref, xr_ref, gr_ref, *rest,
                    da, dr, head_dim, with_prompt_outs):
    x = x_ref[...]
    ms = jnp.mean(x * x, axis=-1, keepdims=True)
    h = (x * lax.rsqrt(ms + EPS) * g_ref[...]).astype(BF16)

    cosf, sin_up, sin_dn = c_ref[...], s1_ref[...], s2_ref[...]
    bd = bd_ref[...]
    half = head_dim // ROPE_FRACTION // 2

    def proj(start, width):
        return jnp.dot(h, w_ref[:, start:start + width], preferred_element_type=F32)

    def head_norm_rope(t, gain):
        sq = t * t
        hi = sq.astype(BF16)
        lo = (sq - hi.astype(F32)).astype(BF16)
        ssum = (jnp.dot(hi, bd, preferred_element_type=F32)
                + jnp.dot(lo, bd, preferred_element_type=F32))
        tn = t * lax.rsqrt(ssum * (1.0 / head_dim) + EPS) * gain
        cols = []
        for c in range(da // LANES):
            tc = tn[:, c * LANES:(c + 1) * LANES]
            up = pltpu.roll(tc, LANES - half, 1)
            dn = pltpu.roll(tc, half, 1)
            cols.append(tc * cosf + up * sin_up + dn * sin_dn)
        return jnp.concatenate(cols, axis=1)

    q_ref[...] = head_norm_rope(proj(0, da), gq_ref[...])
    k = head_norm_rope(proj(da, da), gk_ref[...])
    k_ref[...] = k
    v = proj(2 * da, da)
    v_ref[...] = v
    ga_ref[...] = proj(3 * da, da)
    xr_ref[...] = proj(4 * da, dr)
    gr_ref[...] = proj(4 * da + dr, dr)

    if with_prompt_outs:
        kb_ref, vb_ref, km_ref = rest
        kb_ref[...] = k.astype(BF16)
        vb_ref[...] = v.astype(BF16)
        tm = k.shape[0]
        rows = [jnp.mean(k[b * MOBA_BLOCK:(b + 1) * MOBA_BLOCK], axis=0, keepdims=True)
                for b in range(tm // MOBA_BLOCK)]
        rows.append(jnp.zeros((SUBLANES - len(rows), da), F32))
        km_ref[0] = jnp.concatenate(rows, axis=0)


def _rope_tables(pos, head_dim):
    rope_dim = head_dim // ROPE_FRACTION
    half = rope_dim // 2
    inv = ROPE_THETA ** (-jnp.arange(half, dtype=F32) * 2.0 / rope_dim)
    ang = pos.astype(F32)[:, None] * inv
    cos, sin = jnp.cos(ang), jnp.sin(ang)
    n = pos.shape[0]
    ones = jnp.ones((n, head_dim - rope_dim), F32)
    zeros = jnp.zeros((n, head_dim - rope_dim), F32)
    zh = jnp.zeros((n, half), F32)
    reps = LANES // head_dim
    cosf = jnp.tile(jnp.concatenate([cos, cos, ones], axis=1), (1, reps))
    sin_up = jnp.tile(jnp.concatenate([-sin, zh, zeros], axis=1), (1, reps))
    sin_dn = jnp.tile(jnp.concatenate([zh, sin, zeros], axis=1), (1, reps))
    return cosf, sin_up, sin_dn


def _proj_in(x2d, pos, rms_g, gq, gk, w_bf, bd, *, da, dr, head_dim, tm, with_prompt_outs):
    rows, d = x2d.shape
    n_pos = pos.shape[0]
    assert rows % tm == 0 and n_pos % tm == 0
    pos_tiles = n_pos // tm
    cosf, sin_up, sin_dn = _rope_tables(pos, head_dim)
    reps = da // head_dim

    row_spec = lambda w: pl.BlockSpec((tm, w), lambda i: (i, 0))
    full = lambda a: pl.BlockSpec(a.shape, lambda i: (0,) * a.ndim)
    tab_spec = pl.BlockSpec((tm, LANES), lambda i: (i % pos_tiles, 0))

    g2 = rms_g.reshape(1, d)
    gq2 = jnp.tile(gq, reps).reshape(1, da)
    gk2 = jnp.tile(gk, reps).reshape(1, da)

    out_shape = [jax.ShapeDtypeStruct((rows, da), F32)] * 4 + [jax.ShapeDtypeStruct((rows, dr), F32)] * 2
    out_specs = [row_spec(da)] * 4 + [row_spec(dr)] * 2
    if with_prompt_outs:
        assert tm % MOBA_BLOCK == 0 and tm // MOBA_BLOCK <= SUBLANES
        out_shape += [jax.ShapeDtypeStruct((rows, da), BF16)] * 2
        out_shape += [jax.ShapeDtypeStruct((rows // tm, SUBLANES, da), F32)]
        out_specs += [row_spec(da)] * 2 + [pl.BlockSpec((1, SUBLANES, da), lambda i: (i, 0, 0))]

    kern = functools.partial(_proj_in_kernel, da=da, dr=dr, head_dim=head_dim,
                             with_prompt_outs=with_prompt_outs)
    return pl.pallas_call(
        kern,
        grid=(rows // tm,),
        in_specs=[row_spec(d), full(g2), full(w_bf), full(gq2), full(gk2), full(bd),
                  tab_spec, tab_spec, tab_spec],
        out_specs=out_specs,
        out_shape=out_shape,
        compiler_params=_cparams(("arbitrary",)),
        name="proj_in",
    )(x2d, g2, w_bf, gq2, gk2, bd, cosf, sin_up, sin_dn)


def _proj_out_kernel(x_ref, attn_ref, ga_ref, rnn_ref, w_ref, y_ref, *, da):
    a = (attn_ref[...] * _silu(ga_ref[...])).astype(BF16)
    r = rnn_ref[...].astype(BF16)
    y = jnp.dot(a, w_ref[0:da, :], preferred_element_type=F32)
    y = y + jnp.dot(r, w_ref[da:, :], preferred_element_type=F32)
    y_ref[...] = x_ref[...] + y


def _proj_out(x2d, attn, ga, rnn, w_bf, *, tm):
    rows, d = x2d.shape
    da, dr = attn.shape[1], rnn.shape[1]
    assert rows % tm == 0
    row_spec = lambda w: pl.BlockSpec((tm, w), lambda i: (i, 0))
    return pl.pallas_call(
        functools.partial(_proj_out_kernel, da=da),
        grid=(rows // tm,),
        in_specs=[row_spec(d), row_spec(da), row_spec(da), row_spec(dr),
                  pl.BlockSpec(w_bf.shape, lambda i: (0, 0))],
        out_specs=row_spec(d),
        out_shape=jax.ShapeDtypeStruct((rows, d), F32),
        compiler_params=_cparams(("arbitrary",)),
        name="proj_out",
    )(x2d, attn, ga, rnn, w_bf)


def _lru_coeffs(xc, wax_ref, ba_ref, bx_ref, lam_ref, dr):
    pre = jnp.dot(xc.astype(BF16), wax_ref[...], preferred_element_type=F32)
    r = jax.nn.sigmoid(pre[:, :dr] + ba_ref[...])
    gi = jax.nn.sigmoid(pre[:, dr:] + bx_ref[...])
    z = -lam_ref[...]
    softplus = jnp.maximum(z, 0.0) + jnp.log1p(jnp.exp(-jnp.abs(z)))
    log_a = -LRU_C * r * softplus
    a = jnp.exp(log_a)
    mult = jnp.sqrt(jnp.maximum(1.0 - a * a, 0.0))
    return a, mult * (gi * xc)


def _rnn_prompt_kernel(xr_ref, gr_ref, cw_ref, cb_ref, wax_ref, ba_ref, bx_ref, lam_ref,
                       out_ref, hl_ref, cbuf_ref, xe_ref, hc_ref, *, tr, dr, conv_w):
    t = pl.program_id(1)
    pad = SUBLANES

    @pl.when(t == 0)
    def _():
        xe_ref[0:pad, :] = jnp.zeros((pad, dr), F32)
        hc_ref[...] = jnp.zeros_like(hc_ref)

    x = xr_ref[...]
    xe_ref[pad:pad + tr, :] = x
    y = cb_ref[...] + cw_ref[conv_w - 1:conv_w, :] * x
    for j in range(conv_w - 1):
        back = conv_w - 1 - j
        y = y + cw_ref[j:j + 1, :] * xe_ref[pad - back:pad - back + tr, :]
    cbuf_ref[0] = xe_ref[pad + tr - (conv_w - 1):pad + tr, :]
    xe_ref[0:pad, :] = xe_ref[tr:tr + pad, :]

    a, b = _lru_coeffs(y, wax_ref, ba_ref, bx_ref, lam_ref, dr)

    row = lax.broadcasted_iota(jnp.int32, (tr, dr), 0)
    s = 1
    while s < tr:
        keep = row >= s
        a_sh = pltpu.roll(a, s, 0)
        b_sh = pltpu.roll(b, s, 0)
        b = jnp.where(keep, a * b_sh + b, b)
        a = jnp.where(keep, a * a_sh, a)
        s *= 2
    h = a * hc_ref[...] + b
    h_last = h[tr - 1:tr, :]
    hc_ref[...] = h_last
    hl_ref[0] = h_last
    out_ref[...] = h * _silu(gr_ref[...])


def _rnn_prompt(xr, gr, conv_w, conv_b, wax, b_a, b_x, lam, *, batch, seq, tr):
    rows, dr = xr.shape
    cw = conv_w.shape[0]
    assert seq % tr == 0 and rows == batch * seq and cw - 1 <= SUBLANES
    nt = seq // tr
    row_spec = pl.BlockSpec((tr, dr), lambda b, t: (b * nt + t, 0))
    full = lambda a: pl.BlockSpec(a.shape, lambda b, t: (0,) * a.ndim)
    vec = lambda a: a.reshape(1, dr)
    args = (xr, gr, conv_w, vec(conv_b), wax, vec(b_a), vec(b_x), vec(lam))
    return pl.pallas_call(
        functools.partial(_rnn_prompt_kernel, tr=tr, dr=dr, conv_w=cw),
        grid=(batch, nt),
        in_specs=[row_spec, row_spec] + [full(a) for a in args[2:]],
        out_specs=[row_spec,
                   pl.BlockSpec((1, 1, dr), lambda b, t: (b, 0, 0)),
                   pl.BlockSpec((1, cw - 1, dr), lambda b, t: (b, 0, 0))],
        out_shape=[jax.ShapeDtypeStruct((rows, dr), F32),
                   jax.ShapeDtypeStruct((batch, 1, dr), F32),
                   jax.ShapeDtypeStruct((batch, cw - 1, dr), F32)],
        scratch_shapes=[pltpu.VMEM((SUBLANES + tr, dr), F32), pltpu.VMEM((1, dr), F32)],
        compiler_params=_cparams(("arbitrary", "arbitrary")),
        name="rnn_prompt",
    )(*args)


def _rnn_sample_kernel(xr_ref, gr_ref, cs_ref, h0_ref, cw_ref, cb_ref, wax_ref, ba_ref, bx_ref,
                       lam_ref, out_ref, hl_ref, cs_out_ref, *, dr, conv_w):
    x = xr_ref[...]
    y = cb_ref[...] + cw_ref[conv_w - 1:conv_w, :] * x
    for j in range(conv_w - 1):
        y = y + cw_ref[j:j + 1, :] * cs_ref[:, j * dr:(j + 1) * dr]
    a, b = _lru_coeffs(y, wax_ref, ba_ref, bx_ref, lam_ref, dr)
    h = a * h0_ref[...] + b
    hl_ref[...] = h
    out_ref[...] = h * _silu(gr_ref[...])
    if conv_w > 2:
        cs_out_ref[:, 0:(conv_w - 2) * dr] = cs_ref[:, dr:(conv_w - 1) * dr]
    cs_out_ref[:, (conv_w - 2) * dr:] = x


def _rnn_sample(xr, gr, cs, h0, conv_w, conv_b, wax, b_a, b_x, lam):
    rows, dr = xr.shape
    cw = conv_w.shape[0]
    vec = lambda a: a.reshape(1, dr)
    args = (xr, gr, cs, h0, conv_w, vec(conv_b), wax, vec(b_a), vec(b_x), vec(lam))
    full = lambda a: pl.BlockSpec(a.shape, lambda i: (0,) * a.ndim)
    return pl.pallas_call(
        functools.partial(_rnn_sample_kernel, dr=dr, conv_w=cw),
        grid=(1,),
        in_specs=[full(a) for a in args],
        out_specs=[full(xr), full(xr), full(cs)],
        out_shape=[jax.ShapeDtypeStruct((rows, dr), F32), jax.ShapeDtypeStruct((rows, dr), F32),
                   jax.ShapeDtypeStruct(cs.shape, F32)],
        compiler_params=_cparams(("arbitrary",)),
        name="rnn_sample",
    )(*args)


def _first_max_pick(s, idx, axis, sentinel):
    m = jnp.max(s, axis=axis, keepdims=True)
    first = jnp.min(jnp.where((s == m) & (m > -jnp.inf), idx, sentinel), axis=axis, keepdims=True)
    return idx == first, first


def _moba_kernel(q_ref, kb_ref, vb_ref, km_ref, o_ref, m_sc, acc_sc, *, head_dim, n_blocks):
    i = pl.program_id(2)
    blk = MOBA_BLOCK
    scale = head_dim ** -0.5
    q = q_ref[...]
    lane = lax.broadcasted_iota(jnp.int32, (blk, LANES), 1)
    krow = lax.broadcasted_iota(jnp.int32, (blk, blk), 0)
    kcol = lax.broadcasted_iota(jnp.int32, (blk, blk), 1)
    brow = lax.broadcasted_iota(jnp.int32, (n_blocks, blk), 0)
    km = km_ref[0]
    own = pl.multiple_of(i * blk, blk)
    k_own = kb_ref[pl.ds(own, blk), :]
    v_own = vb_ref[pl.ds(own, blk), :]
    outs = []
    for hh in range(LANES // head_dim):
        in_head = (lane >= hh * head_dim) & (lane < (hh + 1) * head_dim)
        qm = jnp.where(in_head, q, 0.0)
        qs = (qm * scale).astype(BF16)

        sb = lax.dot_general(km, qm, NT_DIMS, precision=lax.Precision.HIGHEST,
                             preferred_element_type=F32)
        sb = jnp.where(brow < i, sb, -jnp.inf)
        sel = jnp.zeros((n_blocks, blk), jnp.bool_)
        for _ in range(min(MOBA_TOPK, n_blocks)):
            pick, _ = _first_max_pick(sb, brow, 0, n_blocks)
            sel = sel | pick
            sb = jnp.where(pick, -jnp.inf, sb)
        bias_t = jnp.where(sel, 0.0, NEG)
        bias_t = jnp.concatenate([bias_t, jnp.full((LANES - n_blocks, blk), NEG, F32)], axis=0)
        qa = jnp.concatenate([qs, bias_t.T.astype(BF16)], axis=1)

        keep = jnp.where(in_head, 1.0, 0.0).astype(BF16)
        fill = jnp.where(in_head, 0.0, 1.0).astype(BF16)

        def with_ones(v):
            return v * keep + fill

        s = lax.dot_general(qs, k_own, NT_DIMS, preferred_element_type=F32)
        s = jnp.where(kcol <= krow, s, NEG)
        m0 = jnp.max(s, axis=1, keepdims=True)
        p = jnp.exp(s - m0)
        m_sc[...] = jnp.broadcast_to(m0, (blk, LANES))
        acc_sc[...] = jnp.dot(p.astype(BF16), with_ones(v_own), preferred_element_type=F32)

        def body(n, carry):
            start = pl.multiple_of(n * blk, blk)
            kn = kb_ref[pl.ds(start, blk), :]
            vn = vb_ref[pl.ds(start, blk), :]
            ka = jnp.concatenate([kn, jnp.where(lane == n, 1.0, 0.0).astype(BF16)], axis=1)
            s = lax.dot_general(qa, ka, NT_DIMS, preferred_element_type=F32)
            m_old = m_sc[...]
            m_new = jnp.maximum(m_old, jnp.max(s, axis=1, keepdims=True))
            alpha = jnp.exp(m_old - m_new)
            p = jnp.exp(s - m_new[:, 0:1])
            acc_sc[...] = alpha * acc_sc[...] + jnp.dot(p.astype(BF16), with_ones(vn),
                                                        preferred_element_type=F32)
            m_sc[...] = m_new
            return carry

        lax.fori_loop(0, i, body, 0)
        acc = acc_sc[...]
        outs.append(acc / pltpu.roll(acc, head_dim, 1))

    out = outs[-1]
    for hh in range(LANES // head_dim - 2, -1, -1):
        out = jnp.where(lane < (hh + 1) * head_dim, outs[hh], out)
    o_ref[...] = out


def _moba_prompt(q, kb, vb, km, *, batch, seq, head_dim):
    rows, da = q.shape
    assert LANES // head_dim == 2 and seq % MOBA_BLOCK == 0 and da % LANES == 0
    nb = seq // MOBA_BLOCK
    assert nb % SUBLANES == 0 and nb <= LANES
    blk = MOBA_BLOCK
    return pl.pallas_call(
        functools.partial(_moba_kernel, head_dim=head_dim, n_blocks=nb),
        grid=(batch, da // LANES, nb),
        in_specs=[pl.BlockSpec((blk, LANES), lambda b, hp, i: (b * nb + i, hp)),
                  pl.BlockSpec((seq, LANES), lambda b, hp, i: (b, hp)),
                  pl.BlockSpec((seq, LANES), lambda b, hp, i: (b, hp)),
                  pl.BlockSpec((1, nb, LANES), lambda b, hp, i: (b, 0, hp))],
        out_specs=pl.BlockSpec((blk, LANES), lambda b, hp, i: (b * nb + i, hp)),
        out_shape=jax.ShapeDtypeStruct((rows, da), F32),
        scratch_shapes=[pltpu.VMEM((blk, LANES), F32), pltpu.VMEM((blk, LANES), F32)],
        compiler_params=_cparams(("arbitrary", "arbitrary", "arbitrary")),
        name="moba_prompt",
    )(q, kb, vb, km)


def _head_diag(n_heads, width, head_dim):
    sub = lax.broadcasted_iota(jnp.int32, (n_heads, width), 0)
    lane = lax.broadcasted_iota(jnp.int32, (n_heads, width), 1)
    return (lane >= sub * head_dim) & (lane < (sub + 1) * head_dim)


def _sample_score_kernel(pt_ref, q_ref, kn_ref, *rest, pps, n_pages, ppb, page, n_heads, head_dim,
                         n_sel):
    k_refs = rest[:pps]
    p_ref, pown_ref, idx_ref, s_sc, bs_sc = rest[pps:]
    j = pl.program_id(1)
    da = n_heads * head_dim
    scale = head_dim ** -0.5
    diag = _head_diag(n_heads, da, head_dim)
    qbd = jnp.where(diag, jnp.broadcast_to(q_ref[0], (n_heads, da)), 0.0)
    qbd_s = (qbd * scale).astype(BF16)

    sums = []
    for ii in range(pps):
        kp = k_refs[ii][0]
        s_sc[j * pps + ii] = lax.dot_general(qbd_s, kp.astype(BF16), NT_DIMS,
                                             preferred_element_type=F32)
        sums.append(jnp.sum(kp, axis=0, keepdims=True))
    bps = pps // ppb
    blk_rows = []
    for b in range(bps):
        acc = sums[b * ppb]
        for u in range(1, ppb):
            acc = acc + sums[b * ppb + u]
        blk_rows.append(acc)
    bs_sc[pl.ds(pl.multiple_of(j * bps, bps), bps), :] = jnp.concatenate(blk_rows, axis=0)

    @pl.when(j == pl.num_programs(1) - 1)
    def _():
        n_blk = n_pages // ppb
        kmean = bs_sc[...] * (1.0 / (ppb * page))
        sb = lax.dot_general(qbd, kmean, NT_DIMS, precision=lax.Precision.HIGHEST,
                             preferred_element_type=F32)
        bl = lax.broadcasted_iota(jnp.int32, (n_heads, n_blk), 1)
        s = s_sc[...]
        pblk = lax.shift_right_logical(
            lax.broadcasted_iota(jnp.int32, s.shape, 0), ppb.bit_length() - 1)
        lane = lax.broadcasted_iota(jnp.int32, (n_heads, LANES), 1)
        sel = jnp.zeros(s.shape, jnp.bool_)
        idx_out = jnp.zeros((n_heads, LANES), jnp.int32)
        for r in range(n_sel):
            pick, first = _first_max_pick(sb, bl, 1, n_blk)
            sb = jnp.where(pick, -jnp.inf, sb)
            sel = sel | (pblk == first[None])
            idx_out = jnp.where(lane == r, first, idx_out)
        s_own = jnp.sum(qbd * kn_ref[0], axis=1, keepdims=True) * scale
        sm = jnp.where(sel, s, -jnp.inf)
        m = jnp.max(jnp.max(sm, axis=0), axis=1, keepdims=True)
        m = jnp.maximum(m, s_own)
        p = jnp.where(sel, jnp.exp(s - m[None]), 0.0)
        p_own = jnp.exp(s_own - m)
        den = jnp.sum(jnp.sum(p, axis=0), axis=1, keepdims=True) + p_own
        inv = 1.0 / den
        p_ref[0] = p * inv[None]
        pown_ref[0] = jnp.broadcast_to(p_own * inv, (n_heads, LANES))
        idx_ref[0] = idx_out


def _sample_value_kernel(pt_ref, sel_ref, p_ref, pown_ref, vn_ref, *rest, n_sel, ppb, n_heads,
                         head_dim):
    v_refs = rest[:-1]
    o_ref = rest[-1]
    d = pl.program_id(0)
    da = n_heads * head_dim
    page = p_ref.shape[-1]
    sub = lax.broadcasted_iota(jnp.int32, (n_heads, page), 0)
    acc = jnp.zeros((n_heads, da), F32)
    for h in range(n_heads):
        for r in range(n_sel):
            s = h * n_sel + r
            blk = sel_ref[d * (n_heads * n_sel) + s]
            for u in range(ppb):
                pp = p_ref[0, blk * ppb + u]
                pp = jnp.where(sub == h, pp, 0.0).astype(BF16)
                vp = v_refs[s * ppb + u][0].astype(BF16)
                acc = acc + jnp.dot(pp, vp, preferred_element_type=F32)
    acc = acc + pown_ref[0][:, 0:1] * vn_ref[0]
    diag = _head_diag(n_heads, da, head_dim)
    o_ref[0] = jnp.sum(jnp.where(diag, acc, 0.0), axis=0, keepdims=True)


def _moba_sample(q, k_new, v_new, ck, cv, page_table, *, n_heads, head_dim):
    db, da = q.shape
    n_pool, page, _ = ck.shape
    n_pages = page_table.shape[1]
    ppb = MOBA_BLOCK // page
    assert ppb >= 1 and ppb & (ppb - 1) == 0 and MOBA_BLOCK % page == 0
    assert (n_pages * page) % MOBA_BLOCK == 0
    n_blk = n_pages // ppb
    n_sel = min(MOBA_TOPK, n_blk)
    assert n_sel > 0 and n_sel <= LANES
    pps = SUBLANES * ppb
    assert n_pages % pps == 0
    pt_flat = page_table.reshape(-1).astype(jnp.int32)
    q3, kn3, vn3 = (a.reshape(db, 1, da) for a in (q, k_new, v_new))

    row3 = pl.BlockSpec((1, 1, da), lambda d, j, pt: (d, 0, 0))
    k_specs = [pl.BlockSpec((1, page, da),
                            lambda d, j, pt, ii=ii: (pt[d * n_pages + j * pps + ii], 0, 0))
               for ii in range(pps)]
    head_spec = pl.BlockSpec((1, n_heads, LANES), lambda d, j, pt: (d, 0, 0))
    p_all, p_own, idx = pl.pallas_call(
        functools.partial(_sample_score_kernel, pps=pps, n_pages=n_pages, ppb=ppb, page=page,
                          n_heads=n_heads, head_dim=head_dim, n_sel=n_sel),
        grid_spec=pltpu.PrefetchScalarGridSpec(
            num_scalar_prefetch=1,
            grid=(db, n_pages // pps),
            in_specs=[row3, row3] + k_specs,
            out_specs=[pl.BlockSpec((1, n_pages, n_heads, page), lambda d, j, pt: (d, 0, 0, 0)),
                       head_spec, head_spec],
            scratch_shapes=[pltpu.VMEM((n_pages, n_heads, page), F32),
                            pltpu.VMEM((n_blk, da), F32)]),
        out_shape=[jax.ShapeDtypeStruct((db, n_pages, n_heads, page), F32),
                   jax.ShapeDtypeStruct((db, n_heads, LANES), F32),
                   jax.ShapeDtypeStruct((db, n_heads, LANES), jnp.int32)],
        compiler_params=_cparams(("arbitrary", "arbitrary")),
        name="sample_score",
    )(pt_flat, q3, kn3, *([ck] * pps))

    sel_flat = idx[:, :, :n_sel].reshape(-1)
    n_fetch = n_heads * n_sel * ppb
    row3b = pl.BlockSpec((1, 1, da), lambda d, pt, sel: (d, 0, 0))
    def v_page(d, pt, sel, s, u):
        d = jnp.minimum(d, db - 1)
        blk = jnp.clip(sel[d * (n_heads * n_sel) + s], 0, n_blk - 1)
        return (pt[d * n_pages + blk * ppb + u], 0, 0)

    v_specs = [pl.BlockSpec((1, page, da), functools.partial(v_page, s=f // ppb, u=f % ppb))
               for f in range(n_fetch)]
    out = pl.pallas_call(
        functools.partial(_sample_value_kernel, n_sel=n_sel, ppb=ppb, n_heads=n_heads,
                          head_dim=head_dim),
        grid_spec=pltpu.PrefetchScalarGridSpec(
            num_scalar_prefetch=2,
            grid=(db,),
            in_specs=[pl.BlockSpec((1, n_pages, n_heads, page), lambda d, pt, sel: (d, 0, 0, 0)),
                      pl.BlockSpec((1, n_heads, LANES), lambda d, pt, sel: (d, 0, 0)),
                      row3b] + v_specs,
            out_specs=row3b),
        out_shape=jax.ShapeDtypeStruct((db, 1, da), F32),
        compiler_params=_cparams(("arbitrary",)),
        name="sample_value",
    )(pt_flat, sel_flat, p_all, p_own, vn3, *([cv] * n_fetch))
    return out.reshape(db, da)


def _block_diag(w):
    n, wi, wo = w.shape
    eye = jnp.eye(n, dtype=w.dtype)
    return (eye[:, None, :, None] * w[:, :, None, :]).reshape(n * wi, n * wo)


def _pick_tile(n, cap):
    t = min(n, cap)
    while n % t:
        t //= 2
    return t


def kernel(x_prompt, x_sample, cache_k, cache_v, state_rglru_h, state_conv, page_table, rms_g,
           q_norm_g, k_norm_g, w_in, conv_w, conv_b, w_a, b_a, w_x, b_x, lru_lambda, w_out):
    depth = w_in.shape[0]
    batch, seq, d = x_prompt.shape
    db, dec_seq, _ = x_sample.shape
    _, n_pool, page, n_heads, head_dim = cache_k.shape
    da = n_heads * head_dim
    dr = state_rglru_h.shape[-1]
    n_pages = page_table.shape[1]
    past_len = n_pages * page
    assert dec_seq == 1, "sample group is one new token per sequence"
    assert w_in.shape[2] == 4 * da + 2 * dr and da % LANES == 0 and dr % LANES == 0

    tm = _pick_tile(seq, 512)
    tms = _pick_tile(db, 512)
    pos_p = jnp.arange(seq)
    pos_s = jnp.full((db,), past_len, jnp.int32)
    bd = _block_diag(jnp.ones((n_heads, head_dim, head_dim), BF16))

    yp = x_prompt.reshape(batch * seq, d)
    ys = x_sample.reshape(db, d)
    outs = {n: [] for n in ("kp", "vp", "hp", "cp", "ks", "vs", "hs", "cs")}
    for l in range(depth):
        w_in_bf = w_in[l].astype(BF16)
        w_out_bf = w_out[l].astype(BF16)
        wax = jnp.concatenate([_block_diag(w_a[l]), _block_diag(w_x[l])], axis=1).astype(BF16)
        proj = functools.partial(_proj_in, rms_g=rms_g[l], gq=q_norm_g[l], gk=k_norm_g[l],
                                 w_bf=w_in_bf, bd=bd, da=da, dr=dr, head_dim=head_dim)
        rnn_w = (conv_w[l], conv_b[l], wax, b_a[l], b_x[l], lru_lambda[l])

        q, k, v, ga, xr, gr, kb, vb, km = proj(yp, pos_p, tm=tm, with_prompt_outs=True)
        km = km[:, :tm // MOBA_BLOCK, :].reshape(batch, seq // MOBA_BLOCK, da)
        attn = _moba_prompt(q, kb, vb, km, batch=batch, seq=seq, head_dim=head_dim)
        rnn, h_last, cbuf = _rnn_prompt(xr, gr, *rnn_w, batch=batch, seq=seq, tr=tm)
        yp = _proj_out(yp, attn, ga, rnn, w_out_bf, tm=tm)
        outs["kp"].append(k.reshape(batch, seq, n_heads, head_dim))
        outs["vp"].append(v.reshape(batch, seq, n_heads, head_dim))
        outs["hp"].append(h_last.reshape(batch, dr))
        outs["cp"].append(cbuf)

        q, k, v, ga, xr, gr = proj(ys, pos_s, tm=tms, with_prompt_outs=False)
        attn = _moba_sample(q, k, v, cache_k[l].reshape(n_pool, page, da),
                            cache_v[l].reshape(n_pool, page, da), page_table,
                            n_heads=n_heads, head_dim=head_dim)
        cs = state_conv[l].reshape(db, -1)
        rnn, h_last, cs_new = _rnn_sample(xr, gr, cs, state_rglru_h[l], *rnn_w)
        ys = _proj_out(ys, attn, ga, rnn, w_out_bf, tm=tms)
        outs["ks"].append(k.reshape(db, 1, n_heads, head_dim))
        outs["vs"].append(v.reshape(db, 1, n_heads, head_dim))
        outs["hs"].append(h_last)
        outs["cs"].append(cs_new.reshape(db, -1, dr))

    st = lambda n: jnp.stack(outs[n])
    return (yp.reshape(batch, seq, d), ys.reshape(db, 1, d), st("kp"), st("vp"), st("hp"),
            st("cp"), st("ks"), st("vs"), st("hs"), st("cs"))
```

```python
import functools

import jax
import jax.numpy as jnp
from jax import lax
from jax.experimental import pallas as pl
from jax.experimental.pallas import tpu as pltpu

ROPE_THETA = 500000.0
ROPE_FRACTION = 4
MOBA_BLOCK = 256
MOBA_TOPK = 3
LRU_C = 8.0
EPS = 1e-6
NEG = -1e30

LANES = 128
SUBLANES = 8
VMEM_LIMIT_BYTES = 56 * 1024 * 1024

F32 = jnp.float32
BF16 = jnp.bfloat16
NT_DIMS = (((1,), (1,)), ((), ()))


def _cparams(sem):
    return pltpu.CompilerParams(dimension_semantics=sem, vmem_limit_bytes=VMEM_LIMIT_BYTES)


def _silu(x):
    return x * jax.nn.sigmoid(x)


def _proj_in_kernel(x_ref, g_ref, w_ref, gq_ref, gk_ref, bd_ref, c_ref, s1_ref, s2_ref,
                    q_ref, k_ref, v_ref, ga_ref, xr_ref, gr_ref, *rest,
                    da, dr, head_dim, with_prompt_outs):
    x = x_ref[...]
    ms = jnp.mean(x * x, axis=-1, keepdims=True)
    h = (x * lax.rsqrt(ms + EPS) * g_ref[...]).astype(BF16)

    cosf, sin_up, sin_dn = c_ref[...], s1_ref[...], s2_ref[...]
    bd = bd_ref[...]
    half = head_dim // ROPE_FRACTION // 2

    def proj(start, width):
        return jnp.dot(h, w_ref[:, start:start + width], preferred_element_type=F32)

    def head_norm_rope(t, gain):
        sq = t * t
        hi = sq.astype(BF16)
        lo = (sq - hi.astype(F32)).astype(BF16)
        ssum = (jnp.dot(hi, bd, preferred_element_type=F32)
                + jnp.dot(lo, bd, preferred_element_type=F32))
        tn = t * lax.rsqrt(ssum * (1.0 / head_dim) + EPS) * gain
        cols = []
        for c in range(da // LANES):
            tc = tn[:, c * LANES:(c + 1) * LANES]
            up = pltpu.roll(tc, LANES - half, 1)
            dn = pltpu.roll(tc, half, 1)
            cols.append(tc * cosf + up * sin_up + dn * sin_dn)
        return jnp.concatenate(cols, axis=1)

    q_ref[...] = head_norm_rope(proj(0, da), gq_ref[...])
    k = head_norm_rope(proj(da, da), gk_ref[...])
    k_ref[...] = k
    v = proj(2 * da, da)
    v_ref[...] = v
    ga_ref[...] = proj(3 * da, da)
    xr_ref[...] = proj(4 * da, dr)
    gr_ref[...] = proj(4 * da + dr, dr)

    if with_prompt_outs:
        kb_ref, vb_ref, km_ref = rest
        kb_ref[...] = k.astype(BF16)
        vb_ref[...] = v.astype(BF16)
        tm = k.shape[0]
        rows = [jnp.mean(k[b * MOBA_BLOCK:(b + 1) * MOBA_BLOCK], axis=0, keepdims=True)
                for b in range(tm // MOBA_BLOCK)]
        rows.append(jnp.zeros((SUBLANES - len(rows), da), F32))
        km_ref[0] = jnp.concatenate(rows, axis=0)


def _rope_tables(pos, head_dim):
    rope_dim = head_dim // ROPE_FRACTION
    half = rope_dim // 2
    inv = ROPE_THETA ** (-jnp.arange(half, dtype=F32) * 2.0 / rope_dim)
    ang = pos.astype(F32)[:, None] * inv
    cos, sin = jnp.cos(ang), jnp.sin(ang)
    n = pos.shape[0]
    ones = jnp.ones((n, head_dim - rope_dim), F32)
    zeros = jnp.zeros((n, head_dim - rope_dim), F32)
    zh = jnp.zeros((n, half), F32)
    reps = LANES // head_dim
    cosf = jnp.tile(jnp.concatenate([cos, cos, ones], axis=1), (1, reps))
    sin_up = jnp.tile(jnp.concatenate([-sin, zh, zeros], axis=1), (1, reps))
    sin_dn = jnp.tile(jnp.concatenate([zh, sin, zeros], axis=1), (1, reps))
    return cosf, sin_up, sin_dn


def _proj_in(x2d, pos, rms_g, gq, gk, w_bf, bd, *, da, dr, head_dim, tm, with_prompt_outs):
    rows, d = x2d.shape
    n_pos = pos.shape[0]
    assert rows % tm == 0 and n_pos % tm == 0
    pos_tiles = n_pos // tm
    cosf, sin_up, sin_dn = _rope_tables(pos, head_dim)
    reps = da // head_dim

    row_spec = lambda w: pl.BlockSpec((tm, w), lambda i: (i, 0))
    full = lambda a: pl.BlockSpec(a.shape, lambda i: (0,) * a.ndim)
    tab_spec = pl.BlockSpec((tm, LANES), lambda i: (i % pos_tiles, 0))

    g2 = rms_g.reshape(1, d)
    gq2 = jnp.tile(gq, reps).reshape(1, da)
    gk2 = jnp.tile(gk, reps).reshape(1, da)

    out_shape = [jax.ShapeDtypeStruct((rows, da), F32)] * 4 + [jax.ShapeDtypeStruct((rows, dr), F32)] * 2
    out_specs = [row_spec(da)] * 4 + [row_spec(dr)] * 2
    if with_prompt_outs:
        assert tm % MOBA_BLOCK == 0 and tm // MOBA_BLOCK <= SUBLANES
        out_shape += [jax.ShapeDtypeStruct((rows, da), BF16)] * 2
        out_shape += [jax.ShapeDtypeStruct((rows // tm, SUBLANES, da), F32)]
        out_specs += [row_spec(da)] * 2 + [pl.BlockSpec((1, SUBLANES, da), lambda i: (i, 0, 0))]

    kern = functools.partial(_proj_in_kernel, da=da, dr=dr, head_dim=head_dim,
                             with_prompt_outs=with_prompt_outs)
    return pl.pallas_call(
        kern,
        grid=(rows // tm,),
        in_specs=[row_spec(d), full(g2), full(w_bf), full(gq2), full(gk2), full(bd),
                  tab_spec, tab_spec, tab_spec],
        out_specs=out_specs,
        out_shape=out_shape,
        compiler_params=_cparams(("arbitrary",)),
        name="proj_in",
    )(x2d, g2, w_bf, gq2, gk2, bd, cosf, sin_up, sin_dn)


def _proj_out_kernel(x_ref, attn_ref, ga_ref, rnn_ref, w_ref, y_ref, *, da):
    a = (attn_ref[...] * _silu(ga_ref[...])).astype(BF16)
    r = rnn_ref[...].astype(BF16)
    y = jnp.dot(a, w_ref[0:da, :], preferred_element_type=F32)
    y = y + jnp.dot(r, w_ref[da:, :], preferred_element_type=F32)
    y_ref[...] = x_ref[...] + y


def _proj_out(x2d, attn, ga, rnn, w_bf, *, tm):
    rows, d = x2d.shape
    da, dr = attn.shape[1], rnn.shape[1]
    assert rows % tm == 0
    row_spec = lambda w: pl.BlockSpec((tm, w), lambda i: (i, 0))
    return pl.pallas_call(
        functools.partial(_proj_out_kernel, da=da),
        grid=(rows // tm,),
        in_specs=[row_spec(d), row_spec(da), row_spec(da), row_spec(dr),
                  pl.BlockSpec(w_bf.shape, lambda i: (0, 0))],
        out_specs=row_spec(d),
        out_shape=jax.ShapeDtypeStruct((rows, d), F32),
        compiler_params=_cparams(("arbitrary",)),
        name="proj_out",
    )(x2d, attn, ga, rnn, w_bf)


def _lru_coeffs(xc, wax_ref, ba_ref, bx_ref, lam_ref, dr):
    pre = jnp.dot(xc.astype(BF16), wax_ref[...], preferred_element_type=F32)
    r = jax.nn.sigmoid(pre[:, :dr] + ba_ref[...])
    gi = jax.nn.sigmoid(pre[:, dr:] + bx_ref[...])
    z = -lam_ref[...]
    softplus = jnp.maximum(z, 0.0) + jnp.log1p(jnp.exp(-jnp.abs(z)))
    log_a = -LRU_C * r * softplus
    a = jnp.exp(log_a)
    mult = jnp.sqrt(jnp.maximum(1.0 - a * a, 0.0))
    return a, mult * (gi * xc)


def _rnn_prompt_kernel(xr_ref, gr_ref, cw_ref, cb_ref, wax_ref, ba_ref, bx_ref, lam_ref,
                       out_ref, hl_ref, cbuf_ref, xe_ref, hc_ref, *, tr, dr, conv_w):
    t = pl.program_id(1)
    pad = SUBLANES

    @pl.when(t == 0)
    def _():
        xe_ref[0:pad, :] = jnp.zeros((pad, dr), F32)
        hc_ref[...] = jnp.zeros_like(hc_ref)

    x = xr_ref[...]
    xe_ref[pad:pad + tr, :] = x
    y = cb_ref[...] + cw_ref[conv_w - 1:conv_w, :] * x
    for j in range(conv_w - 1):
        back = conv_w - 1 - j
        y = y + cw_ref[j:j + 1, :] * xe_ref[pad - back:pad - back + tr, :]
    cbuf_ref[0] = xe_ref[pad + tr - (conv_w - 1):pad + tr, :]
    xe_ref[0:pad, :] = xe_ref[tr:tr + pad, :]

    a, b = _lru_coeffs(y, wax_ref, ba_ref, bx_ref, lam_ref, dr)

    row = lax.broadcasted_iota(jnp.int32, (tr, dr), 0)
    s = 1
    while s < tr:
        keep = row >= s
        a_sh = pltpu.roll(a, s, 0)
        b_sh = pltpu.roll(b, s, 0)
        b = jnp.where(keep, a * b_sh + b, b)
        a = jnp.where(keep, a * a_sh, a)
        s *= 2
    h = a * hc_ref[...] + b
    h_last = h[tr - 1:tr, :]
    hc_ref[...] = h_last
    hl_ref[0] = h_last
    out_ref[...] = h * _silu(gr_ref[...])


def _rnn_prompt(xr, gr, conv_w, conv_b, wax, b_a, b_x, lam, *, batch, seq, tr):
    rows, dr = xr.shape
    cw = conv_w.shape[0]
    assert seq % tr == 0 and rows == batch * seq and cw - 1 <= SUBLANES
    nt = seq // tr
    row_spec = pl.BlockSpec((tr, dr), lambda b, t: (b * nt + t, 0))
    full = lambda a: pl.BlockSpec(a.shape, lambda b, t: (0,) * a.ndim)
    vec = lambda a: a.reshape(1, dr)
    args = (xr, gr, conv_w, vec(conv_b), wax, vec(b_a), vec(b_x), vec(lam))
    return pl.pallas_call(
        functools.partial(_rnn_prompt_kernel, tr=tr, dr=dr, conv_w=cw),
        grid=(batch, nt),
        in_specs=[row_spec, row_spec] + [full(a) for a in args[2:]],
        out_specs=[row_spec,
                   pl.BlockSpec((1, 1, dr), lambda b, t: (b, 0, 0)),
                   pl.BlockSpec((1, cw - 1, dr), lambda b, t: (b, 0, 0))],
        out_shape=[jax.ShapeDtypeStruct((rows, dr), F32),
                   jax.ShapeDtypeStruct((batch, 1, dr), F32),
                   jax.ShapeDtypeStruct((batch, cw - 1, dr), F32)],
        scratch_shapes=[pltpu.VMEM((SUBLANES + tr, dr), F32), pltpu.VMEM((1, dr), F32)],
        compiler_params=_cparams(("arbitrary", "arbitrary")),
        name="rnn_prompt",
    )(*args)


def _rnn_sample_kernel(xr_ref, gr_ref, cs_ref, h0_ref, cw_ref, cb_ref, wax_ref, ba_ref, bx_ref,
                       lam_ref, out_ref, hl_ref, cs_out_ref, *, dr, conv_w):
    x = xr_ref[...]
    y = cb_ref[...] + cw_ref[conv_w - 1:conv_w, :] * x
    for j in range(conv_w - 1):
        y = y + cw_ref[j:j + 1, :] * cs_ref[:, j * dr:(j + 1) * dr]
    a, b = _lru_coeffs(y, wax_ref, ba_ref, bx_ref, lam_ref, dr)
    h = a * h0_ref[...] + b
    hl_ref[...] = h
    out_ref[...] = h * _silu(gr_ref[...])
    if conv_w > 2:
        cs_out_ref[:, 0:(conv_w - 2) * dr] = cs_ref[:, dr:(conv_w - 1) * dr]
    cs_out_ref[:, (conv_w - 2) * dr:] = x


def _rnn_sample(xr, gr, cs, h0, conv_w, conv_b, wax, b_a, b_x, lam):
    rows, dr = xr.shape
    cw = conv_w.shape[0]
    vec = lambda a: a.reshape(1, dr)
    args = (xr, gr, cs, h0, conv_w, vec(conv_b), wax, vec(b_a), vec(b_x), vec(lam))
    full = lambda a: pl.BlockSpec(a.shape, lambda i: (0,) * a.ndim)
    return pl.pallas_call(
        functools.partial(_rnn_sample_kernel, dr=dr, conv_w=cw),
        grid=(1,),
        in_specs=[full(a) for a in args],
        out_specs=[full(xr), full(xr), full(cs)],
        out_shape=[jax.ShapeDtypeStruct((rows, dr), F32), jax.ShapeDtypeStruct((rows, dr), F32),
                   jax.ShapeDtypeStruct(cs.shape, F32)],
        compiler_params=_cparams(("arbitrary",)),
        name="rnn_sample",
    )(*args)


def _first_max_pick(s, idx, axis, sentinel):
    m = jnp.max(s, axis=axis, keepdims=True)
    first = jnp.min(jnp.where((s == m) & (m > -jnp.inf), idx, sentinel), axis=axis, keepdims=True)
    return idx == first, first


def _moba_kernel(q_ref, kb_ref, vb_ref, km_ref, o_ref, m_sc, acc_sc, *, head_dim, n_blocks):
    i = pl.program_id(2)
    blk = MOBA_BLOCK
    scale = head_dim ** -0.5
    q = q_ref[...]
    lane = lax.broadcasted_iota(jnp.int32, (blk, LANES), 1)
    krow = lax.broadcasted_iota(jnp.int32, (blk, blk), 0)
    kcol = lax.broadcasted_iota(jnp.int32, (blk, blk), 1)
    brow = lax.broadcasted_iota(jnp.int32, (n_blocks, blk), 0)
    km = km_ref[0]
    own = pl.multiple_of(i * blk, blk)
    k_own = kb_ref[pl.ds(own, blk), :]
    v_own = vb_ref[pl.ds(own, blk), :]
    outs = []
    for hh in range(LANES // head_dim):
        in_head = (lane >= hh * head_dim) & (lane < (hh + 1) * head_dim)
        qm = jnp.where(in_head, q, 0.0)
        qs = (qm * scale).astype(BF16)

        sb = lax.dot_general(km, qm, NT_DIMS, precision=lax.Precision.HIGHEST,
                             preferred_element_type=F32)
        sb = jnp.where(brow < i, sb, -jnp.inf)
        sel = jnp.zeros((n_blocks, blk), jnp.bool_)
        for _ in range(min(MOBA_TOPK, n_blocks)):
            pick, _ = _first_max_pick(sb, brow, 0, n_blocks)
            sel = sel | pick
            sb = jnp.where(pick, -jnp.inf, sb)
        bias_t = jnp.where(sel, 0.0, NEG)
        bias_t = jnp.concatenate([bias_t, jnp.full((LANES - n_blocks, blk), NEG, F32)], axis=0)
        qa = jnp.concatenate([qs, bias_t.T.astype(BF16)], axis=1)

        keep = jnp.where(in_head, 1.0, 0.0).astype(BF16)
        fill = jnp.where(in_head, 0.0, 1.0).astype(BF16)

        def with_ones(v):
            return v * keep + fill

        s = lax.dot_general(qs, k_own, NT_DIMS, preferred_element_type=F32)
        s = jnp.where(kcol <= krow, s, NEG)
        m0 = jnp.max(s, axis=1, keepdims=True)
        p = jnp.exp(s - m0)
        m_sc[...] = jnp.broadcast_to(m0, (blk, LANES))
        acc_sc[...] = jnp.dot(p.astype(BF16), with_ones(v_own), preferred_element_type=F32)

        def body(n, carry):
            start = pl.multiple_of(n * blk, blk)
            kn = kb_ref[pl.ds(start, blk), :]
            vn = vb_ref[pl.ds(start, blk), :]
            ka = jnp.concatenate([kn, jnp.where(lane == n, 1.0, 0.0).astype(BF16)], axis=1)
            s = lax.dot_general(qa, ka, NT_DIMS, preferred_element_type=F32)
            m_old = m_sc[...]
            m_new = jnp.maximum(m_old, jnp.max(s, axis=1, keepdims=True))
            alpha = jnp.exp(m_old - m_new)
            p = jnp.exp(s - m_new[:, 0:1])
            acc_sc[...] = alpha * acc_sc[...] + jnp.dot(p.astype(BF16), with_ones(vn),
                                                        preferred_element_type=F32)
            m_sc[...] = m_new
            return carry

        lax.fori_loop(0, i, body, 0)
        acc = acc_sc[...]
        outs.append(acc / pltpu.roll(acc, head_dim, 1))

    out = outs[-1]
    for hh in range(LANES // head_dim - 2, -1, -1):
        out = jnp.where(lane < (hh + 1) * head_dim, outs[hh], out)
    o_ref[...] = out


def _moba_prompt(q, kb, vb, km, *, batch, seq, head_dim):
    rows, da = q.shape
    assert LANES // head_dim == 2 and seq % MOBA_BLOCK == 0 and da % LANES == 0
    nb = seq // MOBA_BLOCK
    assert nb % SUBLANES == 0 and nb <= LANES
    blk = MOBA_BLOCK
    return pl.pallas_call(
        functools.partial(_moba_kernel, head_dim=head_dim, n_blocks=nb),
        grid=(batch, da // LANES, nb),
        in_specs=[pl.BlockSpec((blk, LANES), lambda b, hp, i: (b * nb + i, hp)),
                  pl.BlockSpec((seq, LANES), lambda b, hp, i: (b, hp)),
                  pl.BlockSpec((seq, LANES), lambda b, hp, i: (b, hp)),
                  pl.BlockSpec((1, nb, LANES), lambda b, hp, i: (b, 0, hp))],
        out_specs=pl.BlockSpec((blk, LANES), lambda b, hp, i: (b * nb + i, hp)),
        out_shape=jax.ShapeDtypeStruct((rows, da), F32),
        scratch_shapes=[pltpu.VMEM((blk, LANES), F32), pltpu.VMEM((blk, LANES), F32)],
        compiler_params=_cparams(("arbitrary", "arbitrary", "arbitrary")),
        name="moba_prompt",
    )(q, kb, vb, km)


def _head_diag(n_heads, width, head_dim):
    sub = lax.broadcasted_iota(jnp.int32, (n_heads, width), 0)
    lane = lax.broadcasted_iota(jnp.int32, (n_heads, width), 1)
    return (lane >= sub * head_dim) & (lane < (sub + 1) * head_dim)


def _sample_score_kernel(pt_ref, q_ref, kn_ref, *rest, pps, n_pages, ppb, n_heads, head_dim, n_sel):
    k_refs = rest[:pps]
    p_ref, pown_ref, idx_ref, s_sc, qcol_sc = rest[pps:]
    j = pl.program_id(1)
    da = n_heads * head_dim
    page = p_ref.shape[-1]
    scale = head_dim ** -0.5

    @pl.when(j == 0)
    def _():
        qt = jnp.broadcast_to(q_ref[0] * scale, (page, da)).T
        qcol_sc[...] = qt.reshape(n_heads, head_dim, page)

    qcol = qcol_sc[...]
    for ii in range(pps):
        kt = k_refs[ii][0, 0]
        s_sc[j * pps + ii] = jnp.sum(kt * qcol, axis=1)

    @pl.when(j == pl.num_programs(1) - 1)
    def _():
        n_blk = n_pages // ppb
        s = s_sc[...]
        tok = jnp.sum(s, axis=2, keepdims=True)
        sb = jnp.sum(tok.reshape(n_blk, ppb, n_heads, 1), axis=1)
        bidx = lax.broadcasted_iota(jnp.int32, sb.shape, 0)
        pblk = lax.shift_right_logical(
            lax.broadcasted_iota(jnp.int32, s.shape, 0), ppb.bit_length() - 1)
        lane = lax.broadcasted_iota(jnp.int32, (n_heads, LANES), 1)
        sel = jnp.zeros(s.shape, jnp.bool_)
        idx_out = jnp.zeros((n_heads, LANES), jnp.int32)
        for r in range(n_sel):
            pick, first = _first_max_pick(sb, bidx, 0, n_blk)
            sb = jnp.where(pick, -jnp.inf, sb)
            sel = sel | (pblk == first)
            idx_out = jnp.where(lane == r, first[0], idx_out)
        diag = _head_diag(n_heads, da, head_dim)
        qk_new = jnp.broadcast_to(q_ref[0] * kn_ref[0], (n_heads, da))
        s_own = jnp.sum(jnp.where(diag, qk_new, 0.0), axis=1, keepdims=True) * scale
        sm = jnp.where(sel, s, -jnp.inf)
        m = jnp.max(jnp.max(sm, axis=0), axis=1, keepdims=True)
        m = jnp.maximum(m, s_own)
        p = jnp.where(sel, jnp.exp(s - m[None]), 0.0)
        p_own = jnp.exp(s_own - m)
        den = jnp.sum(jnp.sum(p, axis=0), axis=1, keepdims=True) + p_own
        inv = 1.0 / den
        p_ref[0] = p * inv[None]
        pown_ref[0] = jnp.broadcast_to(p_own * inv, (n_heads, LANES))
        idx_ref[0] = idx_out


def _sample_value_kernel(pt_ref, sel_ref, p_ref, pown_ref, vn_ref, *rest, n_sel, ppb, n_heads):
    v_refs = rest[:-1]
    o_ref = rest[-1]
    d = pl.program_id(0)
    page = p_ref.shape[-1]
    head_dim = o_ref.shape[-1]
    sub = lax.broadcasted_iota(jnp.int32, (n_heads, page), 0)
    acc = jnp.zeros((n_heads, head_dim), F32)
    for h in range(n_heads):
        for r in range(n_sel):
            s = h * n_sel + r
            blk = sel_ref[d * (n_heads * n_sel) + s]
            for u in range(ppb):
                pp = p_ref[0, blk * ppb + u]
                pp = jnp.where(sub == h, pp, 0.0).astype(BF16)
                vt = v_refs[s * ppb + u][0, 0, 0].astype(BF16)
                acc = acc + lax.dot_general(pp, vt, NT_DIMS, preferred_element_type=F32)
    o_ref[0] = acc + pown_ref[0][:, 0:1] * vn_ref[0]


def _moba_sample(q, k_new, v_new, kt, vt, page_table, layer, *, n_heads, head_dim):
    db, da = q.shape
    page = kt.shape[-1]
    assert page == LANES
    n_pages = page_table.shape[1]
    ppb = MOBA_BLOCK // page
    assert ppb >= 1 and ppb & (ppb - 1) == 0 and MOBA_BLOCK % page == 0
    assert (n_pages * page) % MOBA_BLOCK == 0
    n_blk = n_pages // ppb
    n_sel = min(MOBA_TOPK, n_blk)
    assert n_sel > 0 and n_sel <= LANES
    pps = SUBLANES * ppb
    assert n_pages % pps == 0
    pt_flat = page_table.reshape(-1).astype(jnp.int32)
    q3, kn3 = (a.reshape(db, 1, da) for a in (q, k_new))
    vn3 = v_new.reshape(db, n_heads, head_dim)

    row3 = pl.BlockSpec((1, 1, da), lambda d, j, pt: (d, 0, 0))
    k_specs = [pl.BlockSpec((1, 1, n_heads, head_dim, page),
                            lambda d, j, pt, ii=ii: (layer, pt[d * n_pages + j * pps + ii], 0, 0, 0))
               for ii in range(pps)]
    head_spec = pl.BlockSpec((1, n_heads, LANES), lambda d, j, pt: (d, 0, 0))
    p_all, p_own, idx = pl.pallas_call(
        functools.partial(_sample_score_kernel, pps=pps, n_pages=n_pages, ppb=ppb,
                          n_heads=n_heads, head_dim=head_dim, n_sel=n_sel),
        grid_spec=pltpu.PrefetchScalarGridSpec(
            num_scalar_prefetch=1,
            grid=(db, n_pages // pps),
            in_specs=[row3, row3] + k_specs,
            out_specs=[pl.BlockSpec((1, n_pages, n_heads, page), lambda d, j, pt: (d, 0, 0, 0)),
                       head_spec, head_spec],
            scratch_shapes=[pltpu.VMEM((n_pages, n_heads, page), F32),
                            pltpu.VMEM((n_heads, head_dim, page), F32)]),
        out_shape=[jax.ShapeDtypeStruct((db, n_pages, n_heads, page), F32),
                   jax.ShapeDtypeStruct((db, n_heads, LANES), F32),
                   jax.ShapeDtypeStruct((db, n_heads, LANES), jnp.int32)],
        compiler_params=_cparams(("arbitrary", "arbitrary")),
        name="sample_score",
    )(pt_flat, q3, kn3, *([kt] * pps))

    sel_flat = idx[:, :, :n_sel].reshape(-1)
    n_fetch = n_heads * n_sel * ppb
    head3 = pl.BlockSpec((1, n_heads, head_dim), lambda d, pt, sel: (d, 0, 0))

    def v_slab(d, pt, sel, s, u):
        d = jnp.minimum(d, db - 1)
        blk = jnp.clip(sel[d * (n_heads * n_sel) + s], 0, n_blk - 1)
        return (layer, pt[d * n_pages + blk * ppb + u], s // n_sel, 0, 0)

    v_specs = [pl.BlockSpec((1, 1, 1, head_dim, page),
                            functools.partial(v_slab, s=f // ppb, u=f % ppb))
               for f in range(n_fetch)]
    out = pl.pallas_call(
        functools.partial(_sample_value_kernel, n_sel=n_sel, ppb=ppb, n_heads=n_heads),
        grid_spec=pltpu.PrefetchScalarGridSpec(
            num_scalar_prefetch=2,
            grid=(db,),
            in_specs=[pl.BlockSpec((1, n_pages, n_heads, page), lambda d, pt, sel: (d, 0, 0, 0)),
                      pl.BlockSpec((1, n_heads, LANES), lambda d, pt, sel: (d, 0, 0)),
                      head3] + v_specs,
            out_specs=head3),
        out_shape=jax.ShapeDtypeStruct((db, n_heads, head_dim), F32),
        compiler_params=_cparams(("arbitrary",)),
        name="sample_value",
    )(pt_flat, sel_flat, p_all, p_own, vn3, *([vt] * n_fetch))
    return out.reshape(db, da)


def _block_diag(w):
    n, wi, wo = w.shape
    eye = jnp.eye(n, dtype=w.dtype)
    return (eye[:, None, :, None] * w[:, :, None, :]).reshape(n * wi, n * wo)


def _pick_tile(n, cap):
    t = min(n, cap)
    while n % t:
        t //= 2
    return t


def kernel(x_prompt, x_sample, cache_k, cache_v, state_rglru_h, state_conv, page_table, rms_g,
           q_norm_g, k_norm_g, w_in, conv_w, conv_b, w_a, b_a, w_x, b_x, lru_lambda, w_out):
    depth = w_in.shape[0]
    batch, seq, d = x_prompt.shape
    db, dec_seq, _ = x_sample.shape
    _, n_pool, page, n_heads, head_dim = cache_k.shape
    da = n_heads * head_dim
    dr = state_rglru_h.shape[-1]
    n_pages = page_table.shape[1]
    past_len = n_pages * page
    assert dec_seq == 1, "sample group is one new token per sequence"
    assert w_in.shape[2] == 4 * da + 2 * dr and da % LANES == 0 and dr % LANES == 0

    tm = _pick_tile(seq, 512)
    tms = _pick_tile(db, 512)
    pos_p = jnp.arange(seq)
    pos_s = jnp.full((db,), past_len, jnp.int32)
    bd = _block_diag(jnp.ones((n_heads, head_dim, head_dim), BF16))
    cache_kt = jnp.transpose(cache_k, (0, 1, 3, 4, 2))
    cache_vt = jnp.transpose(cache_v, (0, 1, 3, 4, 2))

    yp = x_prompt.reshape(batch * seq, d)
    ys = x_sample.reshape(db, d)
    outs = {n: [] for n in ("kp", "vp", "hp", "cp", "ks", "vs", "hs", "cs")}
    for l in range(depth):
        w_in_bf = w_in[l].astype(BF16)
        w_out_bf = w_out[l].astype(BF16)
        wax = jnp.concatenate([_block_diag(w_a[l]), _block_diag(w_x[l])], axis=1).astype(BF16)
        proj = functools.partial(_proj_in, rms_g=rms_g[l], gq=q_norm_g[l], gk=k_norm_g[l],
                                 w_bf=w_in_bf, bd=bd, da=da, dr=dr, head_dim=head_dim)
        rnn_w = (conv_w[l], conv_b[l], wax, b_a[l], b_x[l], lru_lambda[l])

        q, k, v, ga, xr, gr, kb, vb, km = proj(yp, pos_p, tm=tm, with_prompt_outs=True)
        km = km[:, :tm // MOBA_BLOCK, :].reshape(batch, seq // MOBA_BLOCK, da)
        attn = _moba_prompt(q, kb, vb, km, batch=batch, seq=seq, head_dim=head_dim)
        rnn, h_last, cbuf = _rnn_prompt(xr, gr, *rnn_w, batch=batch, seq=seq, tr=tm)
        yp = _proj_out(yp, attn, ga, rnn, w_out_bf, tm=tm)
        outs["kp"].append(k.reshape(batch, seq, n_heads, head_dim))
        outs["vp"].append(v.reshape(batch, seq, n_heads, head_dim))
        outs["hp"].append(h_last.reshape(batch, dr))
        outs["cp"].append(cbuf)

        q, k, v, ga, xr, gr = proj(ys, pos_s, tm=tms, with_prompt_outs=False)
        attn = _moba_sample(q, k, v, cache_kt, cache_vt, page_table, l,
                            n_heads=n_heads, head_dim=head_dim)
        cs = state_conv[l].reshape(db, -1)
        rnn, h_last, cs_new = _rnn_sample(xr, gr, cs, state_rglru_h[l], *rnn_w)
        ys = _proj_out(ys, attn, ga, rnn, w_out_bf, tm=tms)
        outs["ks"].append(k.reshape(db, 1, n_heads, head_dim))
        outs["vs"].append(v.reshape(db, 1, n_heads, head_dim))
        outs["hs"].append(h_last)
        outs["cs"].append(cs_new.reshape(db, -1, dr))

    st = lambda n: jnp.stack(outs[n])
    return (yp.reshape(batch, seq, d), ys.reshape(db, 1, d), st("kp"), st("vp"), st("hp"),
            st("cp"), st("ks"), st("vs"), st("hs"), st("cs"))
```

```python
import functools

import jax
import jax.numpy as jnp
from jax import lax
from jax.experimental import pallas as pl
from jax.experimental.pallas import tpu as pltpu

ROPE_THETA = 500000.0
ROPE_FRACTION = 4
MOBA_BLOCK = 256
MOBA_TOPK = 3
LRU_C = 8.0
EPS = 1e-6
NEG = -1e30
LOG2E = 1.4426950408889634
PAST_TILE_UNROLL = 5

LANES = 128
SUBLANES = 8
VMEM_LIMIT_BYTES = 56 * 1024 * 1024

F32 = jnp.float32
BF16 = jnp.bfloat16
NT_DIMS = (((1,), (1,)), ((), ()))


def _cparams(sem):
    return pltpu.CompilerParams(dimension_semantics=sem, vmem_limit_bytes=VMEM_LIMIT_BYTES)


def _silu(x):
    return x * jax.nn.sigmoid(x)


def _proj_in_kernel(x_ref, g_ref, w_ref, gq_ref, gk_ref, bd_ref, c_ref, s1_ref, s2_ref,
                    q_ref, k_ref, v_ref, ga_ref, xr_ref, gr_ref, *rest,
                    da, dr, head_dim, pos_tiles, with_prompt_outs):
    x = x_ref[...]
    ms = jnp.mean(x * x, axis=-1, keepdims=True)
    h = (x * lax.rsqrt(ms + EPS) * g_ref[...]).astype(BF16)

    cosf, sin_up, sin_dn = c_ref[...], s1_ref[...], s2_ref[...]
    bd = bd_ref[...]
    half = head_dim // ROPE_FRACTION // 2

    def proj(start, width):
        return jnp.dot(h, w_ref[:, start:start + width], preferred_element_type=F32)

    def head_norm_rope(t, gain):
        sq = t * t
        hi = sq.astype(BF16)
        lo = (sq - hi.astype(F32)).astype(BF16)
        ssum = (jnp.dot(hi, bd, preferred_element_type=F32)
                + jnp.dot(lo, bd, preferred_element_type=F32))
        tn = t * lax.rsqrt(ssum * (1.0 / head_dim) + EPS) * gain
        cols = []
        for c in range(da // LANES):
            tc = tn[:, c * LANES:(c + 1) * LANES]
            up = pltpu.roll(tc, LANES - half, 1)
            dn = pltpu.roll(tc, half, 1)
            cols.append(tc * cosf + up * sin_up + dn * sin_dn)
        return jnp.concatenate(cols, axis=1)

    q_ref[...] = head_norm_rope(proj(0, da), gq_ref[...])
    k = head_norm_rope(proj(da, da), gk_ref[...])
    k_ref[...] = k
    v = proj(2 * da, da)
    v_ref[...] = v
    ga_ref[...] = proj(3 * da, da)
    xr_ref[...] = proj(4 * da, dr)
    gr_ref[...] = proj(4 * da + dr, dr)

    if with_prompt_outs:
        ka_ref, vb_ref, km_ref = rest
        tm = k.shape[0]
        vb_ref[...] = v.astype(BF16)
        lane = lax.broadcasted_iota(jnp.int32, (tm, LANES), 1)
        row_blk = lax.shift_right_logical(lax.broadcasted_iota(jnp.int32, (tm, LANES), 0),
                                          MOBA_BLOCK.bit_length() - 1)
        blk_id = (pl.program_id(0) % pos_tiles) * (tm // MOBA_BLOCK) + row_blk
        onehot = jnp.where(lane == blk_id, 1.0, 0.0).astype(BF16)
        kb = k.astype(BF16)
        ka_ref[...] = jnp.concatenate(
            [piece for c in range(da // LANES) for piece in (kb[:, c * LANES:(c + 1) * LANES], onehot)],
            axis=1)
        rows = [jnp.mean(k[b * MOBA_BLOCK:(b + 1) * MOBA_BLOCK], axis=0, keepdims=True)
                for b in range(tm // MOBA_BLOCK)]
        rows.append(jnp.zeros((SUBLANES - len(rows), da), F32))
        km_ref[0] = jnp.concatenate(rows, axis=0)


def _rope_tables(pos, head_dim):
    rope_dim = head_dim // ROPE_FRACTION
    half = rope_dim // 2
    inv = ROPE_THETA ** (-jnp.arange(half, dtype=F32) * 2.0 / rope_dim)
    ang = pos.astype(F32)[:, None] * inv
    cos, sin = jnp.cos(ang), jnp.sin(ang)
    n = pos.shape[0]
    ones = jnp.ones((n, head_dim - rope_dim), F32)
    zeros = jnp.zeros((n, head_dim - rope_dim), F32)
    zh = jnp.zeros((n, half), F32)
    reps = LANES // head_dim
    cosf = jnp.tile(jnp.concatenate([cos, cos, ones], axis=1), (1, reps))
    sin_up = jnp.tile(jnp.concatenate([-sin, zh, zeros], axis=1), (1, reps))
    sin_dn = jnp.tile(jnp.concatenate([zh, sin, zeros], axis=1), (1, reps))
    return cosf, sin_up, sin_dn


def _proj_in(x2d, pos, rms_g, gq, gk, w_bf, bd, *, da, dr, head_dim, tm, with_prompt_outs):
    rows, d = x2d.shape
    n_pos = pos.shape[0]
    assert rows % tm == 0 and n_pos % tm == 0
    pos_tiles = n_pos // tm
    cosf, sin_up, sin_dn = _rope_tables(pos, head_dim)
    reps = da // head_dim

    row_spec = lambda w: pl.BlockSpec((tm, w), lambda i: (i, 0))
    full = lambda a: pl.BlockSpec(a.shape, lambda i: (0,) * a.ndim)
    tab_spec = pl.BlockSpec((tm, LANES), lambda i: (i % pos_tiles, 0))

    g2 = rms_g.reshape(1, d)
    gq2 = jnp.tile(gq, reps).reshape(1, da)
    gk2 = jnp.tile(gk, reps).reshape(1, da)

    out_shape = [jax.ShapeDtypeStruct((rows, da), F32)] * 4 + [jax.ShapeDtypeStruct((rows, dr), F32)] * 2
    out_specs = [row_spec(da)] * 4 + [row_spec(dr)] * 2
    if with_prompt_outs:
        assert tm % MOBA_BLOCK == 0 and tm // MOBA_BLOCK <= SUBLANES
        assert n_pos // MOBA_BLOCK <= LANES and MOBA_BLOCK & (MOBA_BLOCK - 1) == 0
        out_shape += [jax.ShapeDtypeStruct((rows, 2 * da), BF16)]
        out_shape += [jax.ShapeDtypeStruct((rows, da), BF16)]
        out_shape += [jax.ShapeDtypeStruct((rows // tm, SUBLANES, da), F32)]
        out_specs += [row_spec(2 * da), row_spec(da)]
        out_specs += [pl.BlockSpec((1, SUBLANES, da), lambda i: (i, 0, 0))]

    kern = functools.partial(_proj_in_kernel, da=da, dr=dr, head_dim=head_dim, pos_tiles=pos_tiles,
                             with_prompt_outs=with_prompt_outs)
    return pl.pallas_call(
        kern,
        grid=(rows // tm,),
        in_specs=[row_spec(d), full(g2), full(w_bf), full(gq2), full(gk2), full(bd),
                  tab_spec, tab_spec, tab_spec],
        out_specs=out_specs,
        out_shape=out_shape,
        compiler_params=_cparams(("arbitrary",)),
        name="proj_in",
    )(x2d, g2, w_bf, gq2, gk2, bd, cosf, sin_up, sin_dn)


def _proj_out_kernel(x_ref, attn_ref, ga_ref, rnn_ref, w_ref, y_ref, *, da):
    a = (attn_ref[...] * _silu(ga_ref[...])).astype(BF16)
    r = rnn_ref[...].astype(BF16)
    y = jnp.dot(a, w_ref[0:da, :], preferred_element_type=F32)
    y = y + jnp.dot(r, w_ref[da:, :], preferred_element_type=F32)
    y_ref[...] = x_ref[...] + y


def _proj_out(x2d, attn, ga, rnn, w_bf, *, tm):
    rows, d = x2d.shape
    da, dr = attn.shape[1], rnn.shape[1]
    assert rows % tm == 0
    row_spec = lambda w: pl.BlockSpec((tm, w), lambda i: (i, 0))
    return pl.pallas_call(
        functools.partial(_proj_out_kernel, da=da),
        grid=(rows // tm,),
        in_specs=[row_spec(d), row_spec(da), row_spec(da), row_spec(dr),
                  pl.BlockSpec(w_bf.shape, lambda i: (0, 0))],
        out_specs=row_spec(d),
        out_shape=jax.ShapeDtypeStruct((rows, d), F32),
        compiler_params=_cparams(("arbitrary",)),
        name="proj_out",
    )(x2d, attn, ga, rnn, w_bf)


def _lru_coeffs(xc, wax_ref, ba_ref, bx_ref, lam_ref, dr):
    pre = jnp.dot(xc.astype(BF16), wax_ref[...], preferred_element_type=F32)
    r = jax.nn.sigmoid(pre[:, :dr] + ba_ref[...])
    gi = jax.nn.sigmoid(pre[:, dr:] + bx_ref[...])
    z = -lam_ref[...]
    softplus = jnp.maximum(z, 0.0) + jnp.log1p(jnp.exp(-jnp.abs(z)))
    log_a = -LRU_C * r * softplus
    a = jnp.exp(log_a)
    mult = jnp.sqrt(jnp.maximum(1.0 - a * a, 0.0))
    return a, mult * (gi * xc)


def _rnn_prompt_kernel(xr_ref, gr_ref, cw_ref, cb_ref, wax_ref, ba_ref, bx_ref, lam_ref,
                       out_ref, hl_ref, cbuf_ref, xe_ref, hc_ref, *, tr, dr, conv_w):
    t = pl.program_id(1)
    pad = SUBLANES

    @pl.when(t == 0)
    def _():
        xe_ref[0:pad, :] = jnp.zeros((pad, dr), F32)
        hc_ref[...] = jnp.zeros_like(hc_ref)

    x = xr_ref[...]
    xe_ref[pad:pad + tr, :] = x
    y = cb_ref[...] + cw_ref[conv_w - 1:conv_w, :] * x
    for j in range(conv_w - 1):
        back = conv_w - 1 - j
        y = y + cw_ref[j:j + 1, :] * xe_ref[pad - back:pad - back + tr, :]
    cbuf_ref[0] = xe_ref[pad + tr - (conv_w - 1):pad + tr, :]
    xe_ref[0:pad, :] = xe_ref[tr:tr + pad, :]

    a, b = _lru_coeffs(y, wax_ref, ba_ref, bx_ref, lam_ref, dr)

    row = lax.broadcasted_iota(jnp.int32, (tr, dr), 0)
    s = 1
    while s < tr:
        keep = row >= s
        a_sh = pltpu.roll(a, s, 0)
        b_sh = pltpu.roll(b, s, 0)
        b = jnp.where(keep, a * b_sh + b, b)
        a = jnp.where(keep, a * a_sh, a)
        s *= 2
    h = a * hc_ref[...] + b
    h_last = h[tr - 1:tr, :]
    hc_ref[...] = h_last
    hl_ref[0] = h_last
    out_ref[...] = h * _silu(gr_ref[...])


def _rnn_prompt(xr, gr, conv_w, conv_b, wax, b_a, b_x, lam, *, batch, seq, tr):
    rows, dr = xr.shape
    cw = conv_w.shape[0]
    assert seq % tr == 0 and rows == batch * seq and cw - 1 <= SUBLANES
    nt = seq // tr
    row_spec = pl.BlockSpec((tr, dr), lambda b, t: (b * nt + t, 0))
    full = lambda a: pl.BlockSpec(a.shape, lambda b, t: (0,) * a.ndim)
    vec = lambda a: a.reshape(1, dr)
    args = (xr, gr, conv_w, vec(conv_b), wax, vec(b_a), vec(b_x), vec(lam))
    return pl.pallas_call(
        functools.partial(_rnn_prompt_kernel, tr=tr, dr=dr, conv_w=cw),
        grid=(batch, nt),
        in_specs=[row_spec, row_spec] + [full(a) for a in args[2:]],
        out_specs=[row_spec,
                   pl.BlockSpec((1, 1, dr), lambda b, t: (b, 0, 0)),
                   pl.BlockSpec((1, cw - 1, dr), lambda b, t: (b, 0, 0))],
        out_shape=[jax.ShapeDtypeStruct((rows, dr), F32),
                   jax.ShapeDtypeStruct((batch, 1, dr), F32),
                   jax.ShapeDtypeStruct((batch, cw - 1, dr), F32)],
        scratch_shapes=[pltpu.VMEM((SUBLANES + tr, dr), F32), pltpu.VMEM((1, dr), F32)],
        compiler_params=_cparams(("arbitrary", "arbitrary")),
        name="rnn_prompt",
    )(*args)


def _rnn_sample_kernel(xr_ref, gr_ref, cs_ref, h0_ref, cw_ref, cb_ref, wax_ref, ba_ref, bx_ref,
                       lam_ref, out_ref, hl_ref, cs_out_ref, *, dr, conv_w):
    x = xr_ref[...]
    y = cb_ref[...] + cw_ref[conv_w - 1:conv_w, :] * x
    for j in range(conv_w - 1):
        y = y + cw_ref[j:j + 1, :] * cs_ref[:, j * dr:(j + 1) * dr]
    a, b = _lru_coeffs(y, wax_ref, ba_ref, bx_ref, lam_ref, dr)
    h = a * h0_ref[...] + b
    hl_ref[...] = h
    out_ref[...] = h * _silu(gr_ref[...])
    if conv_w > 2:
        cs_out_ref[:, 0:(conv_w - 2) * dr] = cs_ref[:, dr:(conv_w - 1) * dr]
    cs_out_ref[:, (conv_w - 2) * dr:] = x


def _rnn_sample(xr, gr, cs, h0, conv_w, conv_b, wax, b_a, b_x, lam):
    rows, dr = xr.shape
    cw = conv_w.shape[0]
    vec = lambda a: a.reshape(1, dr)
    args = (xr, gr, cs, h0, conv_w, vec(conv_b), wax, vec(b_a), vec(b_x), vec(lam))
    full = lambda a: pl.BlockSpec(a.shape, lambda i: (0,) * a.ndim)
    return pl.pallas_call(
        functools.partial(_rnn_sample_kernel, dr=dr, conv_w=cw),
        grid=(1,),
        in_specs=[full(a) for a in args],
        out_specs=[full(xr), full(xr), full(cs)],
        out_shape=[jax.ShapeDtypeStruct((rows, dr), F32), jax.ShapeDtypeStruct((rows, dr), F32),
                   jax.ShapeDtypeStruct(cs.shape, F32)],
        compiler_params=_cparams(("arbitrary",)),
        name="rnn_sample",
    )(*args)


def _first_max_pick(s, idx, axis, sentinel):
    m = jnp.max(s, axis=axis, keepdims=True)
    first = jnp.min(jnp.where((s == m) & (m > -jnp.inf), idx, sentinel), axis=axis, keepdims=True)
    return idx == first, first


def _bf16_parts(x):
    hi = x.astype(BF16)
    return hi, (x - hi.astype(F32)).astype(BF16)


def _moba_kernel(q_lo_ref, q_hi_ref, ka_ref, vb_ref, km_ref, o_lo_ref, o_hi_ref,
                 s_sc, qa_sc, mx_sc, acc_sc, *, head_dim, n_blocks):
    i = pl.program_id(2)
    blk = MOBA_BLOCK
    qscale = head_dim ** -0.5 * LOG2E
    n_past = n_blocks - 1
    lane = lax.broadcasted_iota(jnp.int32, (blk, LANES), 1)
    n_hh = LANES // head_dim
    future = (lax.broadcasted_iota(jnp.int32, (blk, blk), 1)
              > lax.broadcasted_iota(jnp.int32, (blk, blk), 0))
    brow = lax.broadcasted_iota(jnp.int32, (n_blocks, blk), 0)
    km = km_ref[0]
    q_refs = (q_lo_ref, q_hi_ref)
    qblks = (i, n_blocks - 1 - i)

    def past_tile(j):
        second = j >= i
        start = pl.multiple_of(jnp.where(second, j - i, j) * blk, blk)
        return second.astype(jnp.int32), start

    km_parts = _bf16_parts(km)
    for hh in range(n_hh):
        in_head = (lane >= hh * head_dim) & (lane < (hh + 1) * head_dim)
        for w in range(2):
            qm = jnp.where(in_head, q_refs[w][...], 0.0)
            q_parts = _bf16_parts(qm)
            sb = sum(lax.dot_general(km_parts[a], q_parts[b], NT_DIMS, preferred_element_type=F32)
                     for a, b in ((0, 0), (0, 1), (1, 0)))
            sb = jnp.where(brow < qblks[w], sb, -jnp.inf)
            sel = brow == qblks[w]
            for _ in range(min(MOBA_TOPK, n_blocks)):
                pick, _ = _first_max_pick(sb, brow, 0, n_blocks)
                sel = sel | pick
                sb = jnp.where(pick, -jnp.inf, sb)
            bias_t = jnp.where(sel, 0.0, NEG)
            bias_t = jnp.concatenate([bias_t, jnp.full((LANES - n_blocks, blk), NEG, F32)], axis=0)
            qa_sc[w, hh * blk:(hh + 1) * blk, :] = jnp.concatenate(
                [(qm * qscale).astype(BF16), bias_t.T.astype(BF16)], axis=1)
    mx_sc[...] = jnp.full(mx_sc.shape, NEG, F32)
    acc_sc[...] = jnp.zeros(acc_sc.shape, F32)
    ones = jnp.ones((blk, LANES), BF16)

    def logits(which, start, slot, own):
        ka = ka_ref[pl.ds(start, blk), :]
        for hh in range(n_hh):
            rows = pl.ds(hh * blk, blk)
            s = lax.dot_general(qa_sc[which, rows, :], ka, NT_DIMS, preferred_element_type=F32)
            if own:
                s = jnp.where(future, NEG, s)
            s_sc[slot, rows, :] = s
            mx_sc[which, rows, :] = jnp.maximum(mx_sc[which, rows, :],
                                                jnp.maximum(s[:, :LANES], s[:, LANES:]))

    def weighted_values(which, start, slot):
        va = jnp.concatenate([vb_ref[pl.ds(start, blk), :], ones], axis=1)
        for hh in range(n_hh):
            rows = pl.ds(hh * blk, blk)
            m = mx_sc[which, rows, :]
            s = s_sc[slot, rows, :]
            p = jnp.concatenate([jnp.exp2(s[:, :LANES] - m), jnp.exp2(s[:, LANES:] - m)], axis=1)
            acc_sc[which, rows, :] = acc_sc[which, rows, :] + jnp.dot(
                p.astype(BF16), va, preferred_element_type=F32)

    own_tiles = [(w, pl.multiple_of(qblks[w] * blk, blk), n_past + w) for w in range(2)]

    def past_logits(j, carry):
        logits(*past_tile(j), j, False)
        return carry

    lax.fori_loop(0, n_past, past_logits, 0, unroll=PAST_TILE_UNROLL)
    for which, start, slot in own_tiles:
        logits(which, start, slot, True)
    for w in range(2):
        mx_sc[w] = jnp.broadcast_to(jnp.max(mx_sc[w], axis=1, keepdims=True), mx_sc.shape[1:])

    def past_values(j, carry):
        weighted_values(*past_tile(j), j)
        return carry

    lax.fori_loop(0, n_past, past_values, 0, unroll=PAST_TILE_UNROLL)
    for which, start, slot in own_tiles:
        weighted_values(which, start, slot)

    for w, o_ref in enumerate((o_lo_ref, o_hi_ref)):
        out = None
        for hh in range(n_hh - 1, -1, -1):
            acc = acc_sc[w, hh * blk:(hh + 1) * blk, :]
            o_h = acc[:, :LANES] / acc[:, LANES:]
            out = o_h if out is None else jnp.where(lane < (hh + 1) * head_dim, o_h, out)
        o_ref[...] = out


def _moba_prompt(q, ka, vb, km, *, batch, seq, head_dim):
    rows, da = q.shape
    assert LANES % head_dim == 0 and seq % MOBA_BLOCK == 0 and da % LANES == 0
    nb = seq // MOBA_BLOCK
    assert nb % SUBLANES == 0 and nb <= LANES
    blk = MOBA_BLOCK
    n_hh = LANES // head_dim
    half = nb // 2
    q_spec = lambda f: pl.BlockSpec((blk, LANES), lambda b, hp, i: (b * nb + f(i), hp))
    o_spec = lambda f: pl.BlockSpec((blk, LANES), lambda b, hp, i: (b * half + f(i), hp))
    half_shape = jax.ShapeDtypeStruct((rows // 2, da), F32)
    lo, hi = pl.pallas_call(
        functools.partial(_moba_kernel, head_dim=head_dim, n_blocks=nb),
        grid=(batch, da // LANES, half),
        in_specs=[q_spec(lambda i: i), q_spec(lambda i: nb - 1 - i),
                  pl.BlockSpec((seq, 2 * LANES), lambda b, hp, i: (b, hp)),
                  pl.BlockSpec((seq, LANES), lambda b, hp, i: (b, hp)),
                  pl.BlockSpec((1, nb, LANES), lambda b, hp, i: (b, 0, hp))],
        out_specs=[o_spec(lambda i: i), o_spec(lambda i: half - 1 - i)],
        out_shape=[half_shape, half_shape],
        scratch_shapes=[pltpu.VMEM((nb + 1, n_hh * blk, blk), F32),
                        pltpu.VMEM((2, n_hh * blk, 2 * LANES), BF16),
                        pltpu.VMEM((2, n_hh * blk, LANES), F32),
                        pltpu.VMEM((2, n_hh * blk, 2 * LANES), F32)],
        compiler_params=_cparams(("arbitrary", "arbitrary", "arbitrary")),
        name="moba_prompt",
    )(q, q, ka, vb, km)
    lo = lo.reshape(batch, seq // 2, da)
    hi = hi.reshape(batch, seq // 2, da)
    return jnp.concatenate([lo, hi], axis=1).reshape(rows, da)


def _head_diag(n_heads, width, head_dim):
    sub = lax.broadcasted_iota(jnp.int32, (n_heads, width), 0)
    lane = lax.broadcasted_iota(jnp.int32, (n_heads, width), 1)
    return (lane >= sub * head_dim) & (lane < (sub + 1) * head_dim)


def _sample_score_kernel(pt_ref, q_ref, kn_ref, *rest, pps, n_pages, ppb, n_heads, head_dim, n_sel):
    k_refs = rest[:pps]
    p_ref, pown_ref, idx_ref, s_sc, qcol_sc = rest[pps:]
    j = pl.program_id(1)
    da = n_heads * head_dim
    page = p_ref.shape[-1]
    scale = head_dim ** -0.5

    @pl.when(j == 0)
    def _():
        qt = jnp.broadcast_to(q_ref[0] * scale, (page, da)).T
        qcol_sc[...] = qt.reshape(n_heads, head_dim, page)

    qcol = qcol_sc[...]
    for ii in range(pps):
        kt = k_refs[ii][0, 0]
        s_sc[j * pps + ii] = jnp.sum(kt * qcol, axis=1)

    @pl.when(j == pl.num_programs(1) - 1)
    def _():
        n_blk = n_pages // ppb
        s = s_sc[...]
        tok = jnp.sum(s, axis=2, keepdims=True)
        sb = jnp.sum(tok.reshape(n_blk, ppb, n_heads, 1), axis=1)
        bidx = lax.broadcasted_iota(jnp.int32, sb.shape, 0)
        pblk = lax.shift_right_logical(
            lax.broadcasted_iota(jnp.int32, s.shape, 0), ppb.bit_length() - 1)
        lane = lax.broadcasted_iota(jnp.int32, (n_heads, LANES), 1)
        sel = jnp.zeros(s.shape, jnp.bool_)
        idx_out = jnp.zeros((n_heads, LANES), jnp.int32)
        for r in range(n_sel):
            pick, first = _first_max_pick(sb, bidx, 0, n_blk)
            sb = jnp.where(pick, -jnp.inf, sb)
            sel = sel | (pblk == first)
            idx_out = jnp.where(lane == r, first[0], idx_out)
        diag = _head_diag(n_heads, da, head_dim)
        qk_new = jnp.broadcast_to(q_ref[0] * kn_ref[0], (n_heads, da))
        s_own = jnp.sum(jnp.where(diag, qk_new, 0.0), axis=1, keepdims=True) * scale
        sm = jnp.where(sel, s, -jnp.inf)
        m = jnp.max(jnp.max(sm, axis=0), axis=1, keepdims=True)
        m = jnp.maximum(m, s_own)
        p = jnp.where(sel, jnp.exp(s - m[None]), 0.0)
        p_own = jnp.exp(s_own - m)
        den = jnp.sum(jnp.sum(p, axis=0), axis=1, keepdims=True) + p_own
        inv = 1.0 / den
        p_ref[0] = p * inv[None]
        pown_ref[0] = jnp.broadcast_to(p_own * inv, (n_heads, LANES))
        idx_ref[0] = idx_out


def _sample_value_kernel(pt_ref, sel_ref, p_ref, pown_ref, vn_ref, *rest, n_sel, ppb, n_heads):
    v_refs = rest[:-1]
    o_ref = rest[-1]
    d = pl.program_id(0)
    page = p_ref.shape[-1]
    head_dim = o_ref.shape[-1]
    sub = lax.broadcasted_iota(jnp.int32, (n_heads, page), 0)
    acc = jnp.zeros((n_heads, head_dim), F32)
    for h in range(n_heads):
        for r in range(n_sel):
            s = h * n_sel + r
            blk = sel_ref[d * (n_heads * n_sel) + s]
            for u in range(ppb):
                pp = p_ref[0, blk * ppb + u]
                pp = jnp.where(sub == h, pp, 0.0).astype(BF16)
                vt = v_refs[s * ppb + u][0, 0, 0].astype(BF16)
                acc = acc + lax.dot_general(pp, vt, NT_DIMS, preferred_element_type=F32)
    o_ref[0] = acc + pown_ref[0][:, 0:1] * vn_ref[0]


def _moba_sample(q, k_new, v_new, kt, vt, page_table, layer, *, n_heads, head_dim):
    db, da = q.shape
    page = kt.shape[-1]
    assert page == LANES
    n_pages = page_table.shape[1]
    ppb = MOBA_BLOCK // page
    assert ppb >= 1 and ppb & (ppb - 1) == 0 and MOBA_BLOCK % page == 0
    assert (n_pages * page) % MOBA_BLOCK == 0
    n_blk = n_pages // ppb
    n_sel = min(MOBA_TOPK, n_blk)
    assert n_sel > 0 and n_sel <= LANES
    pps = SUBLANES * ppb
    assert n_pages % pps == 0
    pt_flat = page_table.reshape(-1).astype(jnp.int32)
    q3, kn3 = (a.reshape(db, 1, da) for a in (q, k_new))
    vn3 = v_new.reshape(db, n_heads, head_dim)

    row3 = pl.BlockSpec((1, 1, da), lambda d, j, pt: (d, 0, 0))
    k_specs = [pl.BlockSpec((1, 1, n_heads, head_dim, page),
                            lambda d, j, pt, ii=ii: (layer, pt[d * n_pages + j * pps + ii], 0, 0, 0))
               for ii in range(pps)]
    head_spec = pl.BlockSpec((1, n_heads, LANES), lambda d, j, pt: (d, 0, 0))
    p_all, p_own, idx = pl.pallas_call(
        functools.partial(_sample_score_kernel, pps=pps, n_pages=n_pages, ppb=ppb,
                          n_heads=n_heads, head_dim=head_dim, n_sel=n_sel),
        grid_spec=pltpu.PrefetchScalarGridSpec(
            num_scalar_prefetch=1,
            grid=(db, n_pages // pps),
            in_specs=[row3, row3] + k_specs,
            out_specs=[pl.BlockSpec((1, n_pages, n_heads, page), lambda d, j, pt: (d, 0, 0, 0)),
                       head_spec, head_spec],
            scratch_shapes=[pltpu.VMEM((n_pages, n_heads, page), F32),
                            pltpu.VMEM((n_heads, head_dim, page), F32)]),
        out_shape=[jax.ShapeDtypeStruct((db, n_pages, n_heads, page), F32),
                   jax.ShapeDtypeStruct((db, n_heads, LANES), F32),
                   jax.ShapeDtypeStruct((db, n_heads, LANES), jnp.int32)],
        compiler_params=_cparams(("arbitrary", "arbitrary")),
        name="sample_score",
    )(pt_flat, q3, kn3, *([kt] * pps))

    sel_flat = idx[:, :, :n_sel].reshape(-1)
    n_fetch = n_heads * n_sel * ppb
    head3 = pl.BlockSpec((1, n_heads, head_dim), lambda d, pt, sel: (d, 0, 0))

    def v_slab(d, pt, sel, s, u):
        d = jnp.minimum(d, db - 1)
        blk = jnp.clip(sel[d * (n_heads * n_sel) + s], 0, n_blk - 1)
        return (layer, pt[d * n_pages + blk * ppb + u], s // n_sel, 0, 0)

    v_specs = [pl.BlockSpec((1, 1, 1, head_dim, page),
                            functools.partial(v_slab, s=f // ppb, u=f % ppb))
               for f in range(n_fetch)]
    out = pl.pallas_call(
        functools.partial(_sample_value_kernel, n_sel=n_sel, ppb=ppb, n_heads=n_heads),
        grid_spec=pltpu.PrefetchScalarGridSpec(
            num_scalar_prefetch=2,
            grid=(db,),
            in_specs=[pl.BlockSpec((1, n_pages, n_heads, page), lambda d, pt, sel: (d, 0, 0, 0)),
                      pl.BlockSpec((1, n_heads, LANES), lambda d, pt, sel: (d, 0, 0)),
                      head3] + v_specs,
            out_specs=head3),
        out_shape=jax.ShapeDtypeStruct((db, n_heads, head_dim), F32),
        compiler_params=_cparams(("arbitrary",)),
        name="sample_value",
    )(pt_flat, sel_flat, p_all, p_own, vn3, *([vt] * n_fetch))
    return out.reshape(db, da)


def _block_diag(w):
    n, wi, wo = w.shape
    eye = jnp.eye(n, dtype=w.dtype)
    return (eye[:, None, :, None] * w[:, :, None, :]).reshape(n * wi, n * wo)


def _pick_tile(n, cap):
    t = min(n, cap)
    while n % t:
        t //= 2
    return t


def kernel(x_prompt, x_sample, cache_k, cache_v, state_rglru_h, state_conv, page_table, rms_g,
           q_norm_g, k_norm_g, w_in, conv_w, conv_b, w_a, b_a, w_x, b_x, lru_lambda, w_out):
    depth = w_in.shape[0]
    batch, seq, d = x_prompt.shape
    db, dec_seq, _ = x_sample.shape
    _, n_pool, page, n_heads, head_dim = cache_k.shape
    da = n_heads * head_dim
    dr = state_rglru_h.shape[-1]
    n_pages = page_table.shape[1]
    past_len = n_pages * page
    assert dec_seq == 1, "sample group is one new token per sequence"
    assert w_in.shape[2] == 4 * da + 2 * dr and da % LANES == 0 and dr % LANES == 0

    tm = _pick_tile(seq, 512)
    tms = _pick_tile(db, 512)
    pos_p = jnp.arange(seq)
    pos_s = jnp.full((db,), past_len, jnp.int32)
    bd = _block_diag(jnp.ones((n_heads, head_dim, head_dim), BF16))
    cache_kt = jnp.transpose(cache_k, (0, 1, 3, 4, 2))
    cache_vt = jnp.transpose(cache_v, (0, 1, 3, 4, 2))

    yp = x_prompt.reshape(batch * seq, d)
    ys = x_sample.reshape(db, d)
    outs = {n: [] for n in ("kp", "vp", "hp", "cp", "ks", "vs", "hs", "cs")}
    for l in range(depth):
        w_in_bf = w_in[l].astype(BF16)
        w_out_bf = w_out[l].astype(BF16)
        wax = jnp.concatenate([_block_diag(w_a[l]), _block_diag(w_x[l])], axis=1).astype(BF16)
        proj = functools.partial(_proj_in, rms_g=rms_g[l], gq=q_norm_g[l], gk=k_norm_g[l],
                                 w_bf=w_in_bf, bd=bd, da=da, dr=dr, head_dim=head_dim)
        rnn_w = (conv_w[l], conv_b[l], wax, b_a[l], b_x[l], lru_lambda[l])

        q, k, v, ga, xr, gr, ka, vb, km = proj(yp, pos_p, tm=tm, with_prompt_outs=True)
        km = km[:, :tm // MOBA_BLOCK, :].reshape(batch, seq // MOBA_BLOCK, da)
        attn = _moba_prompt(q, ka, vb, km, batch=batch, seq=seq, head_dim=head_dim)
        rnn, h_last, cbuf = _rnn_prompt(xr, gr, *rnn_w, batch=batch, seq=seq, tr=tm)
        yp = _proj_out(yp, attn, ga, rnn, w_out_bf, tm=tm)
        outs["kp"].append(k.reshape(batch, seq, n_heads, head_dim))
        outs["vp"].append(v.reshape(batch, seq, n_heads, head_dim))
        outs["hp"].append(h_last.reshape(batch, dr))
        outs["cp"].append(cbuf)

        q, k, v, ga, xr, gr = proj(ys, pos_s, tm=tms, with_prompt_outs=False)
        attn = _moba_sample(q, k, v, cache_kt, cache_vt, page_table, l,
                            n_heads=n_heads, head_dim=head_dim)
        cs = state_conv[l].reshape(db, -1)
        rnn, h_last, cs_new = _rnn_sample(xr, gr, cs, state_rglru_h[l], *rnn_w)
        ys = _proj_out(ys, attn, ga, rnn, w_out_bf, tm=tms)
        outs["ks"].append(k.reshape(db, 1, n_heads, head_dim))
        outs["vs"].append(v.reshape(db, 1, n_heads, head_dim))
        outs["hs"].append(h_last)
        outs["cs"].append(cs_new.reshape(db, -1, dr))

    st = lambda n: jnp.stack(outs[n])
    return (yp.reshape(batch, seq, d), ys.reshape(db, 1, d), st("kp"), st("vp"), st("hp"),
            st("cp"), st("ks"), st("vs"), st("hs"), st("cs"))
```

```python
import functools

import jax
import jax.numpy as jnp
from jax import lax
from jax.experimental import pallas as pl
from jax.experimental.pallas import tpu as pltpu

ROPE_THETA = 500000.0
ROPE_FRACTION = 4
MOBA_BLOCK = 256
MOBA_TOPK = 3
LRU_C = 8.0
EPS = 1e-6
NEG = -1e30
LOG2E = 1.4426950408889634
PAST_TILE_UNROLL = 5

LANES = 128
SUBLANES = 8
VMEM_LIMIT_BYTES = 56 * 1024 * 1024

F32 = jnp.float32
BF16 = jnp.bfloat16
NT_DIMS = (((1,), (1,)), ((), ()))


def _cparams(sem):
    return pltpu.CompilerParams(dimension_semantics=sem, vmem_limit_bytes=VMEM_LIMIT_BYTES)


def _silu(x):
    return x * jax.nn.sigmoid(x)


def _proj_in_kernel(x_ref, g_ref, w_ref, gq_ref, gk_ref, bd_ref, c_ref, s1_ref, s2_ref,
                    q_ref, k_ref, v_ref, ga_ref, xr_ref, gr_ref, *rest,
                    da, dr, head_dim, pos_tiles, with_prompt_outs):
    x = x_ref[...]
    ms = jnp.mean(x * x, axis=-1, keepdims=True)
    h = (x * lax.rsqrt(ms + EPS) * g_ref[...]).astype(BF16)

    cosf, sin_up, sin_dn = c_ref[...], s1_ref[...], s2_ref[...]
    bd = bd_ref[...]
    half = head_dim // ROPE_FRACTION // 2

    def proj(start, width):
        return jnp.dot(h, w_ref[:, start:start + width], preferred_element_type=F32)

    def head_norm_rope(t, gain):
        sq = t * t
        hi = sq.astype(BF16)
        lo = (sq - hi.astype(F32)).astype(BF16)
        cols = []
        for c in range(da // LANES):
            grp = slice(c * LANES, (c + 1) * LANES)
            ssum = (jnp.dot(hi[:, grp], bd, preferred_element_type=F32)
                    + jnp.dot(lo[:, grp], bd, preferred_element_type=F32))
            tc = t[:, grp] * lax.rsqrt(ssum * (1.0 / head_dim) + EPS) * gain[:, grp]
            up = pltpu.roll(tc, LANES - half, 1)
            dn = pltpu.roll(tc, half, 1)
            cols.append(tc * cosf + up * sin_up + dn * sin_dn)
        return jnp.concatenate(cols, axis=1)

    q_ref[...] = head_norm_rope(proj(0, da), gq_ref[...])
    k = head_norm_rope(proj(da, da), gk_ref[...])
    k_ref[...] = k
    v = proj(2 * da, da)
    v_ref[...] = v
    ga_ref[...] = proj(3 * da, da)
    xr_ref[...] = proj(4 * da, dr)
    gr_ref[...] = proj(4 * da + dr, dr)

    if with_prompt_outs:
        ka_ref, vb_ref, km_ref = rest
        tm = k.shape[0]
        vb_ref[...] = v.astype(BF16)
        lane = lax.broadcasted_iota(jnp.int32, (tm, LANES), 1)
        row_blk = lax.shift_right_logical(lax.broadcasted_iota(jnp.int32, (tm, LANES), 0),
                                          MOBA_BLOCK.bit_length() - 1)
        blk_id = (pl.program_id(0) % pos_tiles) * (tm // MOBA_BLOCK) + row_blk
        onehot = jnp.where(lane == blk_id, 1.0, 0.0).astype(BF16)
        kb = k.astype(BF16)
        ka_ref[...] = jnp.concatenate(
            [piece for c in range(da // LANES) for piece in (kb[:, c * LANES:(c + 1) * LANES], onehot)],
            axis=1)
        rows = [jnp.mean(k[b * MOBA_BLOCK:(b + 1) * MOBA_BLOCK], axis=0, keepdims=True)
                for b in range(tm // MOBA_BLOCK)]
        rows.append(jnp.zeros((SUBLANES - len(rows), da), F32))
        km_ref[0] = jnp.concatenate(rows, axis=0)


def _rope_tables(pos, head_dim):
    rope_dim = head_dim // ROPE_FRACTION
    half = rope_dim // 2
    inv = ROPE_THETA ** (-jnp.arange(half, dtype=F32) * 2.0 / rope_dim)
    ang = pos.astype(F32)[:, None] * inv
    cos, sin = jnp.cos(ang), jnp.sin(ang)
    n = pos.shape[0]
    ones = jnp.ones((n, head_dim - rope_dim), F32)
    zeros = jnp.zeros((n, head_dim - rope_dim), F32)
    zh = jnp.zeros((n, half), F32)
    reps = LANES // head_dim
    cosf = jnp.tile(jnp.concatenate([cos, cos, ones], axis=1), (1, reps))
    sin_up = jnp.tile(jnp.concatenate([-sin, zh, zeros], axis=1), (1, reps))
    sin_dn = jnp.tile(jnp.concatenate([zh, sin, zeros], axis=1), (1, reps))
    return cosf, sin_up, sin_dn


def _proj_in(x2d, pos, rms_g, gq, gk, w_bf, bd, *, da, dr, head_dim, tm, with_prompt_outs):
    rows, d = x2d.shape
    n_pos = pos.shape[0]
    assert rows % tm == 0 and n_pos % tm == 0
    pos_tiles = n_pos // tm
    cosf, sin_up, sin_dn = _rope_tables(pos, head_dim)
    reps = da // head_dim

    row_spec = lambda w: pl.BlockSpec((tm, w), lambda i: (i, 0))
    full = lambda a: pl.BlockSpec(a.shape, lambda i: (0,) * a.ndim)
    tab_spec = pl.BlockSpec((tm, LANES), lambda i: (i % pos_tiles, 0))

    g2 = rms_g.reshape(1, d)
    gq2 = jnp.tile(gq, reps).reshape(1, da)
    gk2 = jnp.tile(gk, reps).reshape(1, da)

    out_shape = [jax.ShapeDtypeStruct((rows, da), F32)] * 4 + [jax.ShapeDtypeStruct((rows, dr), F32)] * 2
    out_specs = [row_spec(da)] * 4 + [row_spec(dr)] * 2
    if with_prompt_outs:
        assert tm % MOBA_BLOCK == 0 and tm // MOBA_BLOCK <= SUBLANES
        assert n_pos // MOBA_BLOCK <= LANES and MOBA_BLOCK & (MOBA_BLOCK - 1) == 0
        out_shape += [jax.ShapeDtypeStruct((rows, 2 * da), BF16)]
        out_shape += [jax.ShapeDtypeStruct((rows, da), BF16)]
        out_shape += [jax.ShapeDtypeStruct((rows // tm, SUBLANES, da), F32)]
        out_specs += [row_spec(2 * da), row_spec(da)]
        out_specs += [pl.BlockSpec((1, SUBLANES, da), lambda i: (i, 0, 0))]

    kern = functools.partial(_proj_in_kernel, da=da, dr=dr, head_dim=head_dim, pos_tiles=pos_tiles,
                             with_prompt_outs=with_prompt_outs)
    return pl.pallas_call(
        kern,
        grid=(rows // tm,),
        in_specs=[row_spec(d), full(g2), full(w_bf), full(gq2), full(gk2), full(bd),
                  tab_spec, tab_spec, tab_spec],
        out_specs=out_specs,
        out_shape=out_shape,
        compiler_params=_cparams(("arbitrary",)),
        name="proj_in",
    )(x2d, g2, w_bf, gq2, gk2, bd, cosf, sin_up, sin_dn)


def _proj_out_kernel(x_ref, attn_ref, ga_ref, rnn_ref, w_ref, y_ref, *, da):
    a = (attn_ref[...] * _silu(ga_ref[...])).astype(BF16)
    r = rnn_ref[...].astype(BF16)
    y = jnp.dot(a, w_ref[0:da, :], preferred_element_type=F32)
    y = y + jnp.dot(r, w_ref[da:, :], preferred_element_type=F32)
    y_ref[...] = x_ref[...] + y


def _proj_out(x2d, attn, ga, rnn, w_bf, *, tm):
    rows, d = x2d.shape
    da, dr = attn.shape[1], rnn.shape[1]
    assert rows % tm == 0
    row_spec = lambda w: pl.BlockSpec((tm, w), lambda i: (i, 0))
    return pl.pallas_call(
        functools.partial(_proj_out_kernel, da=da),
        grid=(rows // tm,),
        in_specs=[row_spec(d), row_spec(da), row_spec(da), row_spec(dr),
                  pl.BlockSpec(w_bf.shape, lambda i: (0, 0))],
        out_specs=row_spec(d),
        out_shape=jax.ShapeDtypeStruct((rows, d), F32),
        compiler_params=_cparams(("arbitrary",)),
        name="proj_out",
    )(x2d, attn, ga, rnn, w_bf)


def _lru_coeffs(xc, wax_ref, ba_ref, bx_ref, lam_ref, dr):
    pre = jnp.dot(xc.astype(BF16), wax_ref[...], preferred_element_type=F32)
    r = jax.nn.sigmoid(pre[:, :dr] + ba_ref[...])
    gi = jax.nn.sigmoid(pre[:, dr:] + bx_ref[...])
    z = -lam_ref[...]
    softplus = jnp.maximum(z, 0.0) + jnp.log1p(jnp.exp(-jnp.abs(z)))
    log_a = -LRU_C * r * softplus
    a = jnp.exp(log_a)
    mult = jnp.sqrt(jnp.maximum(1.0 - a * a, 0.0))
    return a, mult * (gi * xc)


def _rnn_prompt_kernel(xr_ref, gr_ref, cw_ref, cb_ref, wax_ref, ba_ref, bx_ref, lam_ref,
                       out_ref, hl_ref, cbuf_ref, xe_ref, hc_ref, *, tr, dr, conv_w):
    t = pl.program_id(1)
    pad = SUBLANES

    @pl.when(t == 0)
    def _():
        xe_ref[0:pad, :] = jnp.zeros((pad, dr), F32)
        hc_ref[...] = jnp.zeros_like(hc_ref)

    x = xr_ref[...]
    xe_ref[pad:pad + tr, :] = x
    y = cb_ref[...] + cw_ref[conv_w - 1:conv_w, :] * x
    for j in range(conv_w - 1):
        back = conv_w - 1 - j
        y = y + cw_ref[j:j + 1, :] * xe_ref[pad - back:pad - back + tr, :]
    cbuf_ref[0] = xe_ref[pad + tr - (conv_w - 1):pad + tr, :]
    xe_ref[0:pad, :] = xe_ref[tr:tr + pad, :]

    a, b = _lru_coeffs(y, wax_ref, ba_ref, bx_ref, lam_ref, dr)

    row = lax.broadcasted_iota(jnp.int32, (tr, dr), 0)
    s = 1
    while s < tr:
        keep = row >= s
        a_sh = pltpu.roll(a, s, 0)
        b_sh = pltpu.roll(b, s, 0)
        b = jnp.where(keep, a * b_sh + b, b)
        a = jnp.where(keep, a * a_sh, a)
        s *= 2
    h = a * hc_ref[...] + b
    h_last = h[tr - 1:tr, :]
    hc_ref[...] = h_last
    hl_ref[0] = h_last
    out_ref[...] = h * _silu(gr_ref[...])


def _rnn_prompt(xr, gr, conv_w, conv_b, wax, b_a, b_x, lam, *, batch, seq, tr):
    rows, dr = xr.shape
    cw = conv_w.shape[0]
    assert seq % tr == 0 and rows == batch * seq and cw - 1 <= SUBLANES
    nt = seq // tr
    row_spec = pl.BlockSpec((tr, dr), lambda b, t: (b * nt + t, 0))
    full = lambda a: pl.BlockSpec(a.shape, lambda b, t: (0,) * a.ndim)
    vec = lambda a: a.reshape(1, dr)
    args = (xr, gr, conv_w, vec(conv_b), wax, vec(b_a), vec(b_x), vec(lam))
    return pl.pallas_call(
        functools.partial(_rnn_prompt_kernel, tr=tr, dr=dr, conv_w=cw),
        grid=(batch, nt),
        in_specs=[row_spec, row_spec] + [full(a) for a in args[2:]],
        out_specs=[row_spec,
                   pl.BlockSpec((1, 1, dr), lambda b, t: (b, 0, 0)),
                   pl.BlockSpec((1, cw - 1, dr), lambda b, t: (b, 0, 0))],
        out_shape=[jax.ShapeDtypeStruct((rows, dr), F32),
                   jax.ShapeDtypeStruct((batch, 1, dr), F32),
                   jax.ShapeDtypeStruct((batch, cw - 1, dr), F32)],
        scratch_shapes=[pltpu.VMEM((SUBLANES + tr, dr), F32), pltpu.VMEM((1, dr), F32)],
        compiler_params=_cparams(("arbitrary", "arbitrary")),
        name="rnn_prompt",
    )(*args)


def _rnn_sample_kernel(xr_ref, gr_ref, cs_ref, h0_ref, cw_ref, cb_ref, wax_ref, ba_ref, bx_ref,
                       lam_ref, out_ref, hl_ref, cs_out_ref, *, dr, conv_w):
    x = xr_ref[...]
    y = cb_ref[...] + cw_ref[conv_w - 1:conv_w, :] * x
    for j in range(conv_w - 1):
        y = y + cw_ref[j:j + 1, :] * cs_ref[:, j * dr:(j + 1) * dr]
    a, b = _lru_coeffs(y, wax_ref, ba_ref, bx_ref, lam_ref, dr)
    h = a * h0_ref[...] + b
    hl_ref[...] = h
    out_ref[...] = h * _silu(gr_ref[...])
    if conv_w > 2:
        cs_out_ref[:, 0:(conv_w - 2) * dr] = cs_ref[:, dr:(conv_w - 1) * dr]
    cs_out_ref[:, (conv_w - 2) * dr:] = x


def _rnn_sample(xr, gr, cs, h0, conv_w, conv_b, wax, b_a, b_x, lam):
    rows, dr = xr.shape
    cw = conv_w.shape[0]
    vec = lambda a: a.reshape(1, dr)
    args = (xr, gr, cs, h0, conv_w, vec(conv_b), wax, vec(b_a), vec(b_x), vec(lam))
    full = lambda a: pl.BlockSpec(a.shape, lambda i: (0,) * a.ndim)
    return pl.pallas_call(
        functools.partial(_rnn_sample_kernel, dr=dr, conv_w=cw),
        grid=(1,),
        in_specs=[full(a) for a in args],
        out_specs=[full(xr), full(xr), full(cs)],
        out_shape=[jax.ShapeDtypeStruct((rows, dr), F32), jax.ShapeDtypeStruct((rows, dr), F32),
                   jax.ShapeDtypeStruct(cs.shape, F32)],
        compiler_params=_cparams(("arbitrary",)),
        name="rnn_sample",
    )(*args)


def _first_max_pick(s, idx, axis, sentinel):
    m = jnp.max(s, axis=axis, keepdims=True)
    first = jnp.min(jnp.where((s == m) & (m > -jnp.inf), idx, sentinel), axis=axis, keepdims=True)
    return idx == first, first


def _bf16_parts(x):
    hi = x.astype(BF16)
    return hi, (x - hi.astype(F32)).astype(BF16)


def _moba_kernel(pt_ref, q_lo_ref, q_hi_ref, ka_ref, vb_ref, km_ref, qcol_ref, *rest,
                 head_dim, n_blocks, pages_per_step):
    page_refs = rest[:pages_per_step]
    o_lo_ref, o_hi_ref, s_page_ref, s_sc, qa_sc, mx_sc, acc_sc = rest[pages_per_step:]
    _page_logits(page_refs, qcol_ref, s_page_ref, head_dim ** -0.5)

    i = pl.program_id(2)
    blk = MOBA_BLOCK
    qscale = head_dim ** -0.5 * LOG2E
    n_past = n_blocks - 1
    lane = lax.broadcasted_iota(jnp.int32, (blk, LANES), 1)
    n_hh = LANES // head_dim
    future = (lax.broadcasted_iota(jnp.int32, (blk, blk), 1)
              > lax.broadcasted_iota(jnp.int32, (blk, blk), 0))
    brow = lax.broadcasted_iota(jnp.int32, (n_blocks, blk), 0)
    km = km_ref[0]
    q_refs = (q_lo_ref, q_hi_ref)
    qblks = (i, n_blocks - 1 - i)

    def past_tile(j):
        second = j >= i
        start = pl.multiple_of(jnp.where(second, j - i, j) * blk, blk)
        return second.astype(jnp.int32), start

    km_parts = _bf16_parts(km)
    for hh in range(n_hh):
        in_head = (lane >= hh * head_dim) & (lane < (hh + 1) * head_dim)
        for w in range(2):
            qm = jnp.where(in_head, q_refs[w][...], 0.0)
            q_parts = _bf16_parts(qm)
            sb = sum(lax.dot_general(km_parts[a], q_parts[b], NT_DIMS, preferred_element_type=F32)
                     for a, b in ((0, 0), (0, 1), (1, 0)))
            sb = jnp.where(brow < qblks[w], sb, -jnp.inf)
            sel = brow == qblks[w]
            for _ in range(min(MOBA_TOPK, n_blocks)):
                pick, _ = _first_max_pick(sb, brow, 0, n_blocks)
                sel = sel | pick
                sb = jnp.where(pick, -jnp.inf, sb)
            bias_t = jnp.where(sel, 0.0, NEG)
            bias_t = jnp.concatenate([bias_t, jnp.full((LANES - n_blocks, blk), NEG, F32)], axis=0)
            qa_sc[w, hh * blk:(hh + 1) * blk, :] = jnp.concatenate(
                [(qm * qscale).astype(BF16), bias_t.T.astype(BF16)], axis=1)
    mx_sc[...] = jnp.full(mx_sc.shape, NEG, F32)
    acc_sc[...] = jnp.zeros(acc_sc.shape, F32)
    ones = jnp.ones((blk, LANES), BF16)

    def logits(which, start, slot, own):
        ka = ka_ref[pl.ds(start, blk), :]
        for hh in range(n_hh):
            rows = pl.ds(hh * blk, blk)
            s = lax.dot_general(qa_sc[which, rows, :], ka, NT_DIMS, preferred_element_type=F32)
            if own:
                s = jnp.where(future, NEG, s)
            s_sc[slot, rows, :] = s
            mx_sc[which, rows, :] = jnp.maximum(mx_sc[which, rows, :],
                                                jnp.maximum(s[:, :LANES], s[:, LANES:]))

    def weighted_values(which, start, slot):
        va = jnp.concatenate([vb_ref[pl.ds(start, blk), :], ones], axis=1)
        for hh in range(n_hh):
            rows = pl.ds(hh * blk, blk)
            m = mx_sc[which, rows, :]
            s = s_sc[slot, rows, :]
            p = jnp.concatenate([jnp.exp2(s[:, :LANES] - m), jnp.exp2(s[:, LANES:] - m)], axis=1)
            acc_sc[which, rows, :] = acc_sc[which, rows, :] + jnp.dot(
                p.astype(BF16), va, preferred_element_type=F32)

    own_tiles = [(w, pl.multiple_of(qblks[w] * blk, blk), n_past + w) for w in range(2)]

    def past_logits(j, carry):
        logits(*past_tile(j), j, False)
        return carry

    lax.fori_loop(0, n_past, past_logits, 0, unroll=PAST_TILE_UNROLL)
    for which, start, slot in own_tiles:
        logits(which, start, slot, True)
    for w in range(2):
        mx_sc[w] = jnp.broadcast_to(jnp.max(mx_sc[w], axis=1, keepdims=True), mx_sc.shape[1:])

    def past_values(j, carry):
        weighted_values(*past_tile(j), j)
        return carry

    lax.fori_loop(0, n_past, past_values, 0, unroll=PAST_TILE_UNROLL)
    for which, start, slot in own_tiles:
        weighted_values(which, start, slot)

    for w, o_ref in enumerate((o_lo_ref, o_hi_ref)):
        out = None
        for hh in range(n_hh - 1, -1, -1):
            acc = acc_sc[w, hh * blk:(hh + 1) * blk, :]
            o_h = acc[:, :LANES] / acc[:, LANES:]
            out = o_h if out is None else jnp.where(lane < (hh + 1) * head_dim, o_h, out)
        o_ref[...] = out


def _moba_prompt(q, ka, vb, km, q_s, kt, pt_flat, layer, *, batch, seq, n_pages, head_dim):
    rows, da = q.shape
    assert LANES % head_dim == 0 and seq % MOBA_BLOCK == 0 and da % LANES == 0
    nb = seq // MOBA_BLOCK
    assert nb % SUBLANES == 0 and nb <= LANES
    blk = MOBA_BLOCK
    n_hh = LANES // head_dim
    half = nb // 2
    n_hp = da // LANES

    db = q_s.shape[0]
    n_heads, page = kt.shape[2], kt.shape[4]
    assert page == LANES
    n_steps = batch * n_hp * half
    total_pages = db * n_pages
    assert total_pages % n_steps == 0
    pps = total_pages // n_steps
    assert n_pages % pps == 0
    steps_per_seq = n_pages // pps
    qcol = jnp.broadcast_to(q_s.reshape(db, n_heads, head_dim, 1), (db, n_heads, head_dim, page))

    step = lambda b, hp, i: (b * n_hp + hp) * half + i
    q_spec = lambda f: pl.BlockSpec((blk, LANES), lambda b, hp, i, pt: (b * nb + f(i), hp))
    o_spec = lambda f: pl.BlockSpec((blk, LANES), lambda b, hp, i, pt: (b * half + f(i), hp))
    page_specs = [
        pl.BlockSpec((1, 1, n_heads, head_dim, page),
                     lambda b, hp, i, pt, ii=ii:
                     (layer, pt[jnp.minimum(step(b, hp, i) * pps + ii, total_pages - 1)], 0, 0, 0))
        for ii in range(pps)]
    half_shape = jax.ShapeDtypeStruct((rows // 2, da), F32)
    lo, hi, s_all = pl.pallas_call(
        functools.partial(_moba_kernel, head_dim=head_dim, n_blocks=nb, pages_per_step=pps),
        grid_spec=pltpu.PrefetchScalarGridSpec(
            num_scalar_prefetch=1,
            grid=(batch, n_hp, half),
            in_specs=[q_spec(lambda i: i), q_spec(lambda i: nb - 1 - i),
                      pl.BlockSpec((seq, 2 * LANES), lambda b, hp, i, pt: (b, hp)),
                      pl.BlockSpec((seq, LANES), lambda b, hp, i, pt: (b, hp)),
                      pl.BlockSpec((1, nb, LANES), lambda b, hp, i, pt: (b, 0, hp)),
                      pl.BlockSpec((1, n_heads, head_dim, page),
                                   lambda b, hp, i, pt: (step(b, hp, i) // steps_per_seq, 0, 0, 0))]
            + page_specs,
            out_specs=[o_spec(lambda i: i), o_spec(lambda i: half - 1 - i),
                       pl.BlockSpec((1, pps, n_heads, page),
                                    lambda b, hp, i, pt: (step(b, hp, i), 0, 0, 0))],
            scratch_shapes=[pltpu.VMEM((nb + 1, n_hh * blk, blk), F32),
                            pltpu.VMEM((2, n_hh * blk, 2 * LANES), BF16),
                            pltpu.VMEM((2, n_hh * blk, LANES), F32),
                            pltpu.VMEM((2, n_hh * blk, 2 * LANES), F32)]),
        out_shape=[half_shape, half_shape,
                   jax.ShapeDtypeStruct((n_steps, pps, n_heads, page), F32)],
        compiler_params=_cparams(("arbitrary", "arbitrary", "arbitrary")),
        name="moba_prompt",
    )(pt_flat, q, q, ka, vb, km, qcol, *([kt] * pps))
    lo = lo.reshape(batch, seq // 2, da)
    hi = hi.reshape(batch, seq // 2, da)
    attn = jnp.concatenate([lo, hi], axis=1).reshape(rows, da)
    return attn, s_all.reshape(db, n_pages, n_heads, page)


def _head_diag(n_heads, width, head_dim):
    sub = lax.broadcasted_iota(jnp.int32, (n_heads, width), 0)
    lane = lax.broadcasted_iota(jnp.int32, (n_heads, width), 1)
    return (lane >= sub * head_dim) & (lane < (sub + 1) * head_dim)


def _page_logits(k_refs, qcol_ref, s_ref, scale):
    qcol = qcol_ref[0]
    for ii, k_ref in enumerate(k_refs):
        s_ref[0, ii] = jnp.sum(k_ref[0, 0] * qcol, axis=1) * scale


def _sample_select_kernel(s_ref, q_ref, kn_ref, p_ref, pown_ref, idx_ref, *, ppb, n_heads, head_dim,
                          n_sel):
    da = n_heads * head_dim
    scale = head_dim ** -0.5
    s = s_ref[0]
    n_blk = s.shape[0] // ppb
    tok = jnp.sum(s, axis=2, keepdims=True)
    sb = jnp.sum(tok.reshape(n_blk, ppb, n_heads, 1), axis=1)
    bidx = lax.broadcasted_iota(jnp.int32, sb.shape, 0)
    pblk = lax.shift_right_logical(
        lax.broadcasted_iota(jnp.int32, s.shape, 0), ppb.bit_length() - 1)
    lane = lax.broadcasted_iota(jnp.int32, (n_heads, LANES), 1)
    sel = jnp.zeros(s.shape, jnp.bool_)
    idx_out = jnp.zeros((n_heads, LANES), jnp.int32)
    for r in range(n_sel):
        pick, first = _first_max_pick(sb, bidx, 0, n_blk)
        sb = jnp.where(pick, -jnp.inf, sb)
        sel = sel | (pblk == first)
        idx_out = jnp.where(lane == r, first[0], idx_out)
    diag = _head_diag(n_heads, da, head_dim)
    qk_new = jnp.broadcast_to(q_ref[0] * kn_ref[0], (n_heads, da))
    s_own = jnp.sum(jnp.where(diag, qk_new, 0.0), axis=1, keepdims=True) * scale
    sm = jnp.where(sel, s, -jnp.inf)
    m = jnp.max(jnp.max(sm, axis=0), axis=1, keepdims=True)
    m = jnp.maximum(m, s_own)
    p = jnp.where(sel, jnp.exp(s - m[None]), 0.0)
    p_own = jnp.exp(s_own - m)
    den = jnp.sum(jnp.sum(p, axis=0), axis=1, keepdims=True) + p_own
    inv = 1.0 / den
    p_ref[0] = p * inv[None]
    pown_ref[0] = jnp.broadcast_to(p_own * inv, (n_heads, LANES))
    idx_ref[0] = idx_out


def _sample_value_kernel(pt_ref, sel_ref, p_ref, pown_ref, vn_ref, *rest, n_sel, ppb, n_heads):
    v_refs = rest[:-1]
    o_ref = rest[-1]
    d = pl.program_id(0)
    page = p_ref.shape[-1]
    head_dim = o_ref.shape[-1]
    sub = lax.broadcasted_iota(jnp.int32, (n_heads, page), 0)
    acc = jnp.zeros((n_heads, head_dim), F32)
    for h in range(n_heads):
        for r in range(n_sel):
            s = h * n_sel + r
            blk = sel_ref[d * (n_heads * n_sel) + s]
            for u in range(ppb):
                pp = p_ref[0, blk * ppb + u]
                pp = jnp.where(sub == h, pp, 0.0).astype(BF16)
                vt = v_refs[s * ppb + u][0, 0, 0].astype(BF16)
                acc = acc + lax.dot_general(pp, vt, NT_DIMS, preferred_element_type=F32)
    o_ref[0] = acc + pown_ref[0][:, 0:1] * vn_ref[0]


def _moba_sample(s_all, q, k_new, v_new, vt, pt_flat, layer, *, n_heads, head_dim):
    db, da = q.shape
    n_pages, page = s_all.shape[1], s_all.shape[3]
    ppb = MOBA_BLOCK // page
    assert ppb >= 1 and ppb & (ppb - 1) == 0 and MOBA_BLOCK % page == 0
    assert (n_pages * page) % MOBA_BLOCK == 0
    n_blk = n_pages // ppb
    n_sel = min(MOBA_TOPK, n_blk)
    assert n_sel > 0 and n_sel <= LANES
    q3, kn3 = (a.reshape(db, 1, da) for a in (q, k_new))
    vn3 = v_new.reshape(db, n_heads, head_dim)

    row3 = pl.BlockSpec((1, 1, da), lambda d: (d, 0, 0))
    head_spec = pl.BlockSpec((1, n_heads, LANES), lambda d: (d, 0, 0))
    s_spec = pl.BlockSpec((1, n_pages, n_heads, page), lambda d: (d, 0, 0, 0))
    p_all, p_own, idx = pl.pallas_call(
        functools.partial(_sample_select_kernel, ppb=ppb, n_heads=n_heads, head_dim=head_dim,
                          n_sel=n_sel),
        grid=(db,),
        in_specs=[s_spec, row3, row3],
        out_specs=[s_spec, head_spec, head_spec],
        out_shape=[jax.ShapeDtypeStruct(s_all.shape, F32),
                   jax.ShapeDtypeStruct((db, n_heads, LANES), F32),
                   jax.ShapeDtypeStruct((db, n_heads, LANES), jnp.int32)],
        compiler_params=_cparams(("arbitrary",)),
        name="sample_select",
    )(s_all, q3, kn3)

    sel_flat = idx[:, :, :n_sel].reshape(-1)
    n_fetch = n_heads * n_sel * ppb
    head3 = pl.BlockSpec((1, n_heads, head_dim), lambda d, pt, sel: (d, 0, 0))

    def v_slab(d, pt, sel, s, u):
        d = jnp.minimum(d, db - 1)
        blk = jnp.clip(sel[d * (n_heads * n_sel) + s], 0, n_blk - 1)
        return (layer, pt[d * n_pages + blk * ppb + u], s // n_sel, 0, 0)

    v_specs = [pl.BlockSpec((1, 1, 1, head_dim, page),
                            functools.partial(v_slab, s=f // ppb, u=f % ppb))
               for f in range(n_fetch)]
    out = pl.pallas_call(
        functools.partial(_sample_value_kernel, n_sel=n_sel, ppb=ppb, n_heads=n_heads),
        grid_spec=pltpu.PrefetchScalarGridSpec(
            num_scalar_prefetch=2,
            grid=(db,),
            in_specs=[pl.BlockSpec((1, n_pages, n_heads, page), lambda d, pt, sel: (d, 0, 0, 0)),
                      pl.BlockSpec((1, n_heads, LANES), lambda d, pt, sel: (d, 0, 0)),
                      head3] + v_specs,
            out_specs=head3),
        out_shape=jax.ShapeDtypeStruct((db, n_heads, head_dim), F32),
        compiler_params=_cparams(("arbitrary",)),
        name="sample_value",
    )(pt_flat, sel_flat, p_all, p_own, vn3, *([vt] * n_fetch))
    return out.reshape(db, da)


def _block_diag(w):
    n, wi, wo = w.shape
    eye = jnp.eye(n, dtype=w.dtype)
    return (eye[:, None, :, None] * w[:, :, None, :]).reshape(n * wi, n * wo)


def _pick_tile(n, cap):
    t = min(n, cap)
    while n % t:
        t //= 2
    return t


def kernel(x_prompt, x_sample, cache_k, cache_v, state_rglru_h, state_conv, page_table, rms_g,
           q_norm_g, k_norm_g, w_in, conv_w, conv_b, w_a, b_a, w_x, b_x, lru_lambda, w_out):
    depth = w_in.shape[0]
    batch, seq, d = x_prompt.shape
    db, dec_seq, _ = x_sample.shape
    _, n_pool, page, n_heads, head_dim = cache_k.shape
    da = n_heads * head_dim
    dr = state_rglru_h.shape[-1]
    n_pages = page_table.shape[1]
    past_len = n_pages * page
    assert dec_seq == 1, "sample group is one new token per sequence"
    assert w_in.shape[2] == 4 * da + 2 * dr and da % LANES == 0 and dr % LANES == 0

    tm = _pick_tile(seq, 512)
    tms = _pick_tile(db, 512)
    pos_p = jnp.arange(seq)
    pos_s = jnp.full((db,), past_len, jnp.int32)
    bd = _block_diag(jnp.ones((LANES // head_dim, head_dim, head_dim), BF16))
    cache_kt = jnp.transpose(cache_k, (0, 1, 3, 4, 2))
    cache_vt = jnp.transpose(cache_v, (0, 1, 3, 4, 2))
    pt_flat = page_table.reshape(-1).astype(jnp.int32)

    yp = x_prompt.reshape(batch * seq, d)
    ys = x_sample.reshape(db, d)
    outs = {n: [] for n in ("kp", "vp", "hp", "cp", "ks", "vs", "hs", "cs")}
    for l in range(depth):
        w_in_bf = w_in[l].astype(BF16)
        w_out_bf = w_out[l].astype(BF16)
        wax = jnp.concatenate([_block_diag(w_a[l]), _block_diag(w_x[l])], axis=1).astype(BF16)
        proj = functools.partial(_proj_in, rms_g=rms_g[l], gq=q_norm_g[l], gk=k_norm_g[l],
                                 w_bf=w_in_bf, bd=bd, da=da, dr=dr, head_dim=head_dim)
        rnn_w = (conv_w[l], conv_b[l], wax, b_a[l], b_x[l], lru_lambda[l])

        q, k, v, ga, xr, gr, ka, vb, km = proj(yp, pos_p, tm=tm, with_prompt_outs=True)
        q_s, k_s, v_s, ga_s, xr_s, gr_s = proj(ys, pos_s, tm=tms, with_prompt_outs=False)

        km = km[:, :tm // MOBA_BLOCK, :].reshape(batch, seq // MOBA_BLOCK, da)
        attn, s_all = _moba_prompt(q, ka, vb, km, q_s, cache_kt, pt_flat, l, batch=batch, seq=seq,
                                   n_pages=n_pages, head_dim=head_dim)
        rnn, h_last, cbuf = _rnn_prompt(xr, gr, *rnn_w, batch=batch, seq=seq, tr=tm)
        yp = _proj_out(yp, attn, ga, rnn, w_out_bf, tm=tm)
        outs["kp"].append(k.reshape(batch, seq, n_heads, head_dim))
        outs["vp"].append(v.reshape(batch, seq, n_heads, head_dim))
        outs["hp"].append(h_last.reshape(batch, dr))
        outs["cp"].append(cbuf)

        q, k, v, ga, xr, gr = q_s, k_s, v_s, ga_s, xr_s, gr_s
        attn = _moba_sample(s_all, q, k, v, cache_vt, pt_flat, l, n_heads=n_heads, head_dim=head_dim)
        cs = state_conv[l].reshape(db, -1)
        rnn, h_last, cs_new = _rnn_sample(xr, gr, cs, state_rglru_h[l], *rnn_w)
        ys = _proj_out(ys, attn, ga, rnn, w_out_bf, tm=tms)
        outs["ks"].append(k.reshape(db, 1, n_heads, head_dim))
        outs["vs"].append(v.reshape(db, 1, n_heads, head_dim))
        outs["hs"].append(h_last)
        outs["cs"].append(cs_new.reshape(db, -1, dr))

    st = lambda n: jnp.stack(outs[n])
    return (yp.reshape(batch, seq, d), ys.reshape(db, 1, d), st("kp"), st("vp"), st("hp"),
            st("cp"), st("ks"), st("vs"), st("hs"), st("cs"))
```

```python
import functools

import jax
import jax.numpy as jnp
from jax import lax
from jax.experimental import pallas as pl
from jax.experimental.pallas import tpu as pltpu

ROPE_THETA = 500000.0
ROPE_FRACTION = 4
MOBA_BLOCK = 256
MOBA_TOPK = 3
LRU_C = 8.0
EPS = 1e-6
NEG = -1e30
LOG2E = 1.4426950408889634

LANES = 128
SUBLANES = 8
VMEM_LIMIT_BYTES = 56 * 1024 * 1024

F32 = jnp.float32
BF16 = jnp.bfloat16
NT_DIMS = (((1,), (1,)), ((), ()))


def _cparams(sem):
    return pltpu.CompilerParams(dimension_semantics=sem, vmem_limit_bytes=VMEM_LIMIT_BYTES)


def _silu(x):
    return x * jax.nn.sigmoid(x)


def _proj_in_kernel(x_ref, g_ref, w_ref, gq_ref, gk_ref, bd_ref, c_ref, s1_ref, s2_ref,
                    q_ref, k_ref, v_ref, ga_ref, xr_ref, gr_ref, *rest,
                    da, dr, head_dim, pos_tiles, with_prompt_outs):
    x = x_ref[...]
    ms = jnp.mean(x * x, axis=-1, keepdims=True)
    h = (x * lax.rsqrt(ms + EPS) * g_ref[...]).astype(BF16)

    cosf, sin_up, sin_dn = c_ref[...], s1_ref[...], s2_ref[...]
    bd = bd_ref[...]
    half = head_dim // ROPE_FRACTION // 2

    def proj(start, width):
        return jnp.dot(h, w_ref[:, start:start + width], preferred_element_type=F32)

    def head_norm_rope(t, gain):
        sq = t * t
        hi = sq.astype(BF16)
        lo = (sq - hi.astype(F32)).astype(BF16)
        cols = []
        for c in range(da // LANES):
            grp = slice(c * LANES, (c + 1) * LANES)
            ssum = (jnp.dot(hi[:, grp], bd, preferred_element_type=F32)
                    + jnp.dot(lo[:, grp], bd, preferred_element_type=F32))
            tc = t[:, grp] * lax.rsqrt(ssum * (1.0 / head_dim) + EPS) * gain[:, grp]
            up = pltpu.roll(tc, LANES - half, 1)
            dn = pltpu.roll(tc, half, 1)
            cols.append(tc * cosf + up * sin_up + dn * sin_dn)
        return jnp.concatenate(cols, axis=1)

    q_ref[...] = head_norm_rope(proj(0, da), gq_ref[...])
    k = head_norm_rope(proj(da, da), gk_ref[...])
    v = proj(2 * da, da)
    if with_prompt_outs:
        k_ref[0] = k.T
        v_ref[0] = v.T
    else:
        k_ref[...] = k
        v_ref[...] = v
    ga_ref[...] = proj(3 * da, da)
    xr_ref[...] = proj(4 * da, dr)
    gr_ref[...] = proj(4 * da + dr, dr)

    if with_prompt_outs:
        ka_ref, vb_ref, km_ref = rest
        tm = k.shape[0]
        vb_ref[...] = v.astype(BF16)
        lane = lax.broadcasted_iota(jnp.int32, (tm, LANES), 1)
        row_blk = lax.shift_right_logical(lax.broadcasted_iota(jnp.int32, (tm, LANES), 0),
                                          MOBA_BLOCK.bit_length() - 1)
        blk_id = (pl.program_id(0) % pos_tiles) * (tm // MOBA_BLOCK) + row_blk
        onehot = jnp.where(lane == blk_id, 1.0, 0.0).astype(BF16)
        kb = k.astype(BF16)
        ka_ref[...] = jnp.concatenate(
            [piece for c in range(da // LANES) for piece in (kb[:, c * LANES:(c + 1) * LANES], onehot)],
            axis=1)
        rows = [jnp.mean(k[b * MOBA_BLOCK:(b + 1) * MOBA_BLOCK], axis=0, keepdims=True)
                for b in range(tm // MOBA_BLOCK)]
        rows.append(jnp.zeros((SUBLANES - len(rows), da), F32))
        km_ref[0] = jnp.concatenate(rows, axis=0)


def _rope_tables(pos, head_dim):
    rope_dim = head_dim // ROPE_FRACTION
    half = rope_dim // 2
    inv = ROPE_THETA ** (-jnp.arange(half, dtype=F32) * 2.0 / rope_dim)
    ang = pos.astype(F32)[:, None] * inv
    cos, sin = jnp.cos(ang), jnp.sin(ang)
    n = pos.shape[0]
    ones = jnp.ones((n, head_dim - rope_dim), F32)
    zeros = jnp.zeros((n, head_dim - rope_dim), F32)
    zh = jnp.zeros((n, half), F32)
    reps = LANES // head_dim
    cosf = jnp.tile(jnp.concatenate([cos, cos, ones], axis=1), (1, reps))
    sin_up = jnp.tile(jnp.concatenate([-sin, zh, zeros], axis=1), (1, reps))
    sin_dn = jnp.tile(jnp.concatenate([zh, sin, zeros], axis=1), (1, reps))
    return cosf, sin_up, sin_dn


def _proj_in(x2d, pos, rms_g, gq, gk, w_bf, bd, *, da, dr, head_dim, tm, with_prompt_outs):
    rows, d = x2d.shape
    n_pos = pos.shape[0]
    assert rows % tm == 0 and n_pos % tm == 0
    pos_tiles = n_pos // tm
    cosf, sin_up, sin_dn = _rope_tables(pos, head_dim)
    reps = da // head_dim

    row_spec = lambda w: pl.BlockSpec((tm, w), lambda i: (i, 0))
    full = lambda a: pl.BlockSpec(a.shape, lambda i: (0,) * a.ndim)
    tab_spec = pl.BlockSpec((tm, LANES), lambda i: (i % pos_tiles, 0))

    g2 = rms_g.reshape(1, d)
    gq2 = jnp.tile(gq, reps).reshape(1, da)
    gk2 = jnp.tile(gk, reps).reshape(1, da)

    out_shape = [jax.ShapeDtypeStruct((rows, da), F32)] * 4 + [jax.ShapeDtypeStruct((rows, dr), F32)] * 2
    out_specs = [row_spec(da)] * 4 + [row_spec(dr)] * 2
    if with_prompt_outs:
        out_shape[1:3] = [jax.ShapeDtypeStruct((rows // n_pos, da, n_pos), F32)] * 2
        out_specs[1:3] = [pl.BlockSpec((1, da, tm), lambda i: (i // pos_tiles, 0, i % pos_tiles))] * 2
        assert tm % MOBA_BLOCK == 0 and tm // MOBA_BLOCK <= SUBLANES
        assert n_pos // MOBA_BLOCK <= LANES and MOBA_BLOCK & (MOBA_BLOCK - 1) == 0
        out_shape += [jax.ShapeDtypeStruct((rows, 2 * da), BF16)]
        out_shape += [jax.ShapeDtypeStruct((rows, da), BF16)]
        out_shape += [jax.ShapeDtypeStruct((rows // tm, SUBLANES, da), F32)]
        out_specs += [row_spec(2 * da), row_spec(da)]
        out_specs += [pl.BlockSpec((1, SUBLANES, da), lambda i: (i, 0, 0))]

    kern = functools.partial(_proj_in_kernel, da=da, dr=dr, head_dim=head_dim, pos_tiles=pos_tiles,
                             with_prompt_outs=with_prompt_outs)
    return pl.pallas_call(
        kern,
        grid=(rows // tm,),
        in_specs=[row_spec(d), full(g2), full(w_bf), full(gq2), full(gk2), full(bd),
                  tab_spec, tab_spec, tab_spec],
        out_specs=out_specs,
        out_shape=out_shape,
        compiler_params=_cparams(("arbitrary",)),
        name="proj_in",
    )(x2d, g2, w_bf, gq2, gk2, bd, cosf, sin_up, sin_dn)


def _proj_out_kernel(x_ref, attn_ref, ga_ref, rnn_ref, w_ref, y_ref, *, da):
    a = (attn_ref[...] * _silu(ga_ref[...])).astype(BF16)
    r = rnn_ref[...].astype(BF16)
    y = jnp.dot(a, w_ref[0:da, :], preferred_element_type=F32)
    y = y + jnp.dot(r, w_ref[da:, :], preferred_element_type=F32)
    y_ref[...] = x_ref[...] + y


def _proj_out(x2d, attn, ga, rnn, w_bf, *, tm):
    rows, d = x2d.shape
    da, dr = attn.shape[1], rnn.shape[1]
    assert rows % tm == 0
    row_spec = lambda w: pl.BlockSpec((tm, w), lambda i: (i, 0))
    return pl.pallas_call(
        functools.partial(_proj_out_kernel, da=da),
        grid=(rows // tm,),
        in_specs=[row_spec(d), row_spec(da), row_spec(da), row_spec(dr),
                  pl.BlockSpec(w_bf.shape, lambda i: (0, 0))],
        out_specs=row_spec(d),
        out_shape=jax.ShapeDtypeStruct((rows, d), F32),
        compiler_params=_cparams(("arbitrary",)),
        name="proj_out",
    )(x2d, attn, ga, rnn, w_bf)


def _lru_coeffs(xc, wax_ref, ba_ref, bx_ref, lam_ref, dr):
    pre = jnp.dot(xc.astype(BF16), wax_ref[...], preferred_element_type=F32)
    r = jax.nn.sigmoid(pre[:, :dr] + ba_ref[...])
    gi = jax.nn.sigmoid(pre[:, dr:] + bx_ref[...])
    z = -lam_ref[...]
    softplus = jnp.maximum(z, 0.0) + jnp.log1p(jnp.exp(-jnp.abs(z)))
    log_a = -LRU_C * r * softplus
    a = jnp.exp(log_a)
    mult = jnp.sqrt(jnp.maximum(1.0 - a * a, 0.0))
    return a, mult * (gi * xc)


def _rnn_prompt_kernel(xr_ref, gr_ref, cw_ref, cb_ref, wax_ref, ba_ref, bx_ref, lam_ref,
                       out_ref, hl_ref, cbuf_ref, xe_ref, hc_ref, *, tr, dr, conv_w):
    t = pl.program_id(1)
    pad = SUBLANES

    @pl.when(t == 0)
    def _():
        xe_ref[0:pad, :] = jnp.zeros((pad, dr), F32)
        hc_ref[...] = jnp.zeros_like(hc_ref)

    x = xr_ref[...]
    xe_ref[pad:pad + tr, :] = x
    y = cb_ref[...] + cw_ref[conv_w - 1:conv_w, :] * x
    for j in range(conv_w - 1):
        back = conv_w - 1 - j
        y = y + cw_ref[j:j + 1, :] * xe_ref[pad - back:pad - back + tr, :]
    cbuf_ref[0] = xe_ref[pad + tr - (conv_w - 1):pad + tr, :]
    xe_ref[0:pad, :] = xe_ref[tr:tr + pad, :]

    a, b = _lru_coeffs(y, wax_ref, ba_ref, bx_ref, lam_ref, dr)

    grp = SUBLANES
    a = a.reshape(tr // grp, grp, dr)
    b = b.reshape(tr // grp, grp, dr)
    sub = lax.broadcasted_iota(jnp.int32, a.shape, 1)
    s = 1
    while s < grp:
        keep = sub >= s
        a_sh = pltpu.roll(a, s, 1)
        b_sh = pltpu.roll(b, s, 1)
        b = jnp.where(keep, a * b_sh + b, b)
        a = jnp.where(keep, a * a_sh, a)
        s *= 2
    h_prev = hc_ref[...]
    h_groups = []
    for g in range(tr // grp):
        h_g = a[g] * h_prev + b[g]
        h_groups.append(h_g)
        h_prev = h_g[grp - 1:grp, :]
    h = jnp.concatenate(h_groups, axis=0)
    h_last = h_prev
    hc_ref[...] = h_last
    hl_ref[0] = h_last
    out_ref[...] = h * _silu(gr_ref[...])


def _rnn_prompt(xr, gr, conv_w, conv_b, wax, b_a, b_x, lam, *, batch, seq, tr):
    rows, dr = xr.shape
    cw = conv_w.shape[0]
    assert seq % tr == 0 and rows == batch * seq and cw - 1 <= SUBLANES
    nt = seq // tr
    row_spec = pl.BlockSpec((tr, dr), lambda b, t: (b * nt + t, 0))
    full = lambda a: pl.BlockSpec(a.shape, lambda b, t: (0,) * a.ndim)
    vec = lambda a: a.reshape(1, dr)
    args = (xr, gr, conv_w, vec(conv_b), wax, vec(b_a), vec(b_x), vec(lam))
    return pl.pallas_call(
        functools.partial(_rnn_prompt_kernel, tr=tr, dr=dr, conv_w=cw),
        grid=(batch, nt),
        in_specs=[row_spec, row_spec] + [full(a) for a in args[2:]],
        out_specs=[row_spec,
                   pl.BlockSpec((1, 1, dr), lambda b, t: (b, 0, 0)),
                   pl.BlockSpec((1, cw - 1, dr), lambda b, t: (b, 0, 0))],
        out_shape=[jax.ShapeDtypeStruct((rows, dr), F32),
                   jax.ShapeDtypeStruct((batch, 1, dr), F32),
                   jax.ShapeDtypeStruct((batch, cw - 1, dr), F32)],
        scratch_shapes=[pltpu.VMEM((SUBLANES + tr, dr), F32), pltpu.VMEM((1, dr), F32)],
        compiler_params=_cparams(("arbitrary", "arbitrary")),
        name="rnn_prompt",
    )(*args)


def _rnn_sample_kernel(xr_ref, gr_ref, cs_ref, h0_ref, cw_ref, cb_ref, wax_ref, ba_ref, bx_ref,
                       lam_ref, out_ref, hl_ref, cs_out_ref, *, dr, conv_w):
    x = xr_ref[...]
    y = cb_ref[...] + cw_ref[conv_w - 1:conv_w, :] * x
    for j in range(conv_w - 1):
        y = y + cw_ref[j:j + 1, :] * cs_ref[:, j * dr:(j + 1) * dr]
    a, b = _lru_coeffs(y, wax_ref, ba_ref, bx_ref, lam_ref, dr)
    h = a * h0_ref[...] + b
    hl_ref[...] = h
    out_ref[...] = h * _silu(gr_ref[...])
    if conv_w > 2:
        cs_out_ref[:, 0:(conv_w - 2) * dr] = cs_ref[:, dr:(conv_w - 1) * dr]
    cs_out_ref[:, (conv_w - 2) * dr:] = x


def _rnn_sample(xr, gr, cs, h0, conv_w, conv_b, wax, b_a, b_x, lam):
    rows, dr = xr.shape
    cw = conv_w.shape[0]
    vec = lambda a: a.reshape(1, dr)
    args = (xr, gr, cs, h0, conv_w, vec(conv_b), wax, vec(b_a), vec(b_x), vec(lam))
    full = lambda a: pl.BlockSpec(a.shape, lambda i: (0,) * a.ndim)
    return pl.pallas_call(
        functools.partial(_rnn_sample_kernel, dr=dr, conv_w=cw),
        grid=(1,),
        in_specs=[full(a) for a in args],
        out_specs=[full(xr), full(xr), full(cs)],
        out_shape=[jax.ShapeDtypeStruct((rows, dr), F32), jax.ShapeDtypeStruct((rows, dr), F32),
                   jax.ShapeDtypeStruct(cs.shape, F32)],
        compiler_params=_cparams(("arbitrary",)),
        name="rnn_sample",
    )(*args)


def _first_max_pick(s, idx, axis, sentinel):
    m = jnp.max(s, axis=axis, keepdims=True)
    first = jnp.min(jnp.where((s == m) & (m > -jnp.inf), idx, sentinel), axis=axis, keepdims=True)
    return idx == first, first


def _bf16_parts(x):
    hi = x.astype(BF16)
    return hi, (x - hi.astype(F32)).astype(BF16)


def _moba_kernel(pt_ref, q_lo_ref, q_hi_ref, ka_ref, vb_ref, km_ref, qcol_ref, *rest,
                 head_dim, n_blocks, pages_per_step):
    page_refs = rest[:pages_per_step]
    o_lo_ref, o_hi_ref, s_page_ref, s_sc, qa_sc, mx_sc, acc_sc = rest[pages_per_step:]

    i = pl.program_id(2)
    blk = MOBA_BLOCK
    qscale = head_dim ** -0.5 * LOG2E
    n_past = n_blocks - 1
    lane = lax.broadcasted_iota(jnp.int32, (blk, LANES), 1)
    n_hh = LANES // head_dim
    future = (lax.broadcasted_iota(jnp.int32, (blk, blk), 1)
              > lax.broadcasted_iota(jnp.int32, (blk, blk), 0))
    brow = lax.broadcasted_iota(jnp.int32, (n_blocks, blk), 0)
    km = km_ref[0]
    q_refs = (q_lo_ref, q_hi_ref)
    qblks = (i, n_blocks - 1 - i)

    def past_tile(j):
        second = j >= i
        start = pl.multiple_of(jnp.where(second, j - i, j) * blk, blk)
        return second.astype(jnp.int32), start

    km_parts = _bf16_parts(km)
    for hh in range(n_hh):
        in_head = (lane >= hh * head_dim) & (lane < (hh + 1) * head_dim)
        for w in range(2):
            qm = jnp.where(in_head, q_refs[w][...], 0.0)
            q_parts = _bf16_parts(qm)
            sb = sum(lax.dot_general(km_parts[a], q_parts[b], NT_DIMS, preferred_element_type=F32)
                     for a, b in ((0, 0), (0, 1), (1, 0)))
            sb = jnp.where(brow < qblks[w], sb, -jnp.inf)
            sel = brow == qblks[w]
            for _ in range(min(MOBA_TOPK, n_blocks)):
                pick, _ = _first_max_pick(sb, brow, 0, n_blocks)
                sel = sel | pick
                sb = jnp.where(pick, -jnp.inf, sb)
            bias_t = jnp.where(sel, 0.0, NEG)
            bias_t = jnp.concatenate([bias_t, jnp.full((LANES - n_blocks, blk), NEG, F32)], axis=0)
            qa_sc[w, hh * blk:(hh + 1) * blk, :] = jnp.concatenate(
                [(qm * qscale).astype(BF16), bias_t.T.astype(BF16)], axis=1)
    mx_sc[...] = jnp.full(mx_sc.shape, NEG, F32)
    acc_sc[...] = jnp.zeros(acc_sc.shape, F32)
    ones = jnp.ones((blk, LANES), BF16)

    def logits(which, start, slot, own):
        ka = ka_ref[pl.ds(start, blk), :]
        for hh in range(n_hh):
            rows = pl.ds(hh * blk, blk)
            s = lax.dot_general(qa_sc[which, rows, :], ka, NT_DIMS, preferred_element_type=F32)
            if own:
                s = jnp.where(future, NEG, s)
            s_sc[slot, rows, :] = s
            mx_sc[which, rows, :] = jnp.maximum(mx_sc[which, rows, :],
                                                jnp.maximum(s[:, :LANES], s[:, LANES:]))

    def weighted_values(which, start, slot):
        va = jnp.concatenate([vb_ref[pl.ds(start, blk), :], ones], axis=1)
        for hh in range(n_hh):
            rows = pl.ds(hh * blk, blk)
            m = mx_sc[which, rows, :]
            s = s_sc[slot, rows, :]
            p = jnp.concatenate([jnp.exp2(s[:, :LANES] - m), jnp.exp2(s[:, LANES:] - m)], axis=1)
            acc_sc[which, rows, :] = acc_sc[which, rows, :] + jnp.dot(
                p.astype(BF16), va, preferred_element_type=F32)

    own_tiles = [(w, pl.multiple_of(qblks[w] * blk, blk), n_past + w) for w in range(2)]

    half_pages = pages_per_step // 2

    def past_pass(tile_fn, pages, first_page):
        @pl.when(i >= 0)
        def _():
            for j in range(n_past):
                tile_fn(*past_tile(j), j)
            _page_logits(pages, qcol_ref, s_page_ref, first_page, head_dim ** -0.5)

    past_pass(lambda which, start, slot: logits(which, start, slot, False),
              page_refs[:half_pages], 0)
    for which, start, slot in own_tiles:
        logits(which, start, slot, True)
    for w in range(2):
        mx_sc[w] = jnp.broadcast_to(jnp.max(mx_sc[w], axis=1, keepdims=True), mx_sc.shape[1:])
    past_pass(weighted_values, page_refs[half_pages:], half_pages)
    for which, start, slot in own_tiles:
        weighted_values(which, start, slot)

    for w, o_ref in enumerate((o_lo_ref, o_hi_ref)):
        out = None
        for hh in range(n_hh - 1, -1, -1):
            acc = acc_sc[w, hh * blk:(hh + 1) * blk, :]
            o_h = acc[:, :LANES] / acc[:, LANES:]
            out = o_h if out is None else jnp.where(lane < (hh + 1) * head_dim, o_h, out)
        o_ref[...] = out


def _moba_prompt(q, ka, vb, km, q_s, kt, pt_flat, layer, *, batch, seq, n_pages, head_dim):
    rows, da = q.shape
    assert LANES % head_dim == 0 and seq % MOBA_BLOCK == 0 and da % LANES == 0
    nb = seq // MOBA_BLOCK
    assert nb % SUBLANES == 0 and nb <= LANES
    blk = MOBA_BLOCK
    n_hh = LANES // head_dim
    half = nb // 2
    n_hp = da // LANES

    db = q_s.shape[0]
    n_heads, page = kt.shape[2], kt.shape[4]
    assert page == LANES
    n_steps = batch * n_hp * half
    total_pages = db * n_pages
    assert total_pages % n_steps == 0
    pps = total_pages // n_steps
    assert n_pages % pps == 0
    steps_per_seq = n_pages // pps
    qcol = jnp.broadcast_to(q_s.reshape(db, n_heads, head_dim, 1), (db, n_heads, head_dim, page))

    step = lambda b, hp, i: (b * n_hp + hp) * half + i
    q_spec = lambda f: pl.BlockSpec((blk, LANES), lambda b, hp, i, pt: (b * nb + f(i), hp))
    o_spec = lambda f: pl.BlockSpec((blk, LANES), lambda b, hp, i, pt: (b * half + f(i), hp))
    page_specs = [
        pl.BlockSpec((1, 1, n_heads, head_dim, page),
                     lambda b, hp, i, pt, ii=ii:
                     (layer, pt[jnp.minimum(step(b, hp, i) * pps + ii, total_pages - 1)], 0, 0, 0))
        for ii in range(pps)]
    half_shape = jax.ShapeDtypeStruct((rows // 2, da), F32)
    lo, hi, s_all = pl.pallas_call(
        functools.partial(_moba_kernel, head_dim=head_dim, n_blocks=nb, pages_per_step=pps),
        grid_spec=pltpu.PrefetchScalarGridSpec(
            num_scalar_prefetch=1,
            grid=(batch, n_hp, half),
            in_specs=[q_spec(lambda i: i), q_spec(lambda i: nb - 1 - i),
                      pl.BlockSpec((seq, 2 * LANES), lambda b, hp, i, pt: (b, hp)),
                      pl.BlockSpec((seq, LANES), lambda b, hp, i, pt: (b, hp)),
                      pl.BlockSpec((1, nb, LANES), lambda b, hp, i, pt: (b, 0, hp)),
                      pl.BlockSpec((1, n_heads, head_dim, page),
                                   lambda b, hp, i, pt: (step(b, hp, i) // steps_per_seq, 0, 0, 0))]
            + page_specs,
            out_specs=[o_spec(lambda i: i), o_spec(lambda i: half - 1 - i),
                       pl.BlockSpec((1, pps, n_heads, page),
                                    lambda b, hp, i, pt: (step(b, hp, i), 0, 0, 0))],
            scratch_shapes=[pltpu.VMEM((nb + 1, n_hh * blk, blk), F32),
                            pltpu.VMEM((2, n_hh * blk, 2 * LANES), BF16),
                            pltpu.VMEM((2, n_hh * blk, LANES), F32),
                            pltpu.VMEM((2, n_hh * blk, 2 * LANES), F32)]),
        out_shape=[half_shape, half_shape,
                   jax.ShapeDtypeStruct((n_steps, pps, n_heads, page), F32)],
        compiler_params=_cparams(("arbitrary", "arbitrary", "arbitrary")),
        name="moba_prompt",
    )(pt_flat, q, q, ka, vb, km, qcol, *([kt] * pps))
    lo = lo.reshape(batch, seq // 2, da)
    hi = hi.reshape(batch, seq // 2, da)
    attn = jnp.concatenate([lo, hi], axis=1).reshape(rows, da)
    return attn, s_all.reshape(db, n_pages, n_heads, page)


def _head_diag(n_heads, width, head_dim):
    sub = lax.broadcasted_iota(jnp.int32, (n_heads, width), 0)
    lane = lax.broadcasted_iota(jnp.int32, (n_heads, width), 1)
    return (lane >= sub * head_dim) & (lane < (sub + 1) * head_dim)


def _page_logits(k_refs, qcol_ref, s_ref, first, scale):
    qcol = qcol_ref[0]
    for ii, k_ref in enumerate(k_refs):
        s_ref[0, first + ii] = jnp.sum(k_ref[0, 0] * qcol, axis=1) * scale


def _sample_select_kernel(s_ref, q_ref, kn_ref, p_ref, pown_ref, idx_ref, *, ppb, n_heads, head_dim,
                          n_sel):
    da = n_heads * head_dim
    scale = head_dim ** -0.5
    s = s_ref[0]
    n_blk = s.shape[0] // ppb
    tok = jnp.sum(s, axis=2, keepdims=True)
    sb = jnp.sum(tok.reshape(n_blk, ppb, n_heads, 1), axis=1)
    bidx = lax.broadcasted_iota(jnp.int32, sb.shape, 0)
    pblk = lax.shift_right_logical(
        lax.broadcasted_iota(jnp.int32, s.shape, 0), ppb.bit_length() - 1)
    lane = lax.broadcasted_iota(jnp.int32, (n_heads, LANES), 1)
    sel = jnp.zeros(s.shape, jnp.bool_)
    idx_out = jnp.zeros((n_heads, LANES), jnp.int32)
    for r in range(n_sel):
        pick, first = _first_max_pick(sb, bidx, 0, n_blk)
        sb = jnp.where(pick, -jnp.inf, sb)
        sel = sel | (pblk == first)
        idx_out = jnp.where(lane == r, first[0], idx_out)
    diag = _head_diag(n_heads, da, head_dim)
    qk_new = jnp.broadcast_to(q_ref[0] * kn_ref[0], (n_heads, da))
    s_own = jnp.sum(jnp.where(diag, qk_new, 0.0), axis=1, keepdims=True) * scale
    sm = jnp.where(sel, s, -jnp.inf)
    m = jnp.max(jnp.max(sm, axis=0), axis=1, keepdims=True)
    m = jnp.maximum(m, s_own)
    p = jnp.where(sel, jnp.exp(s - m[None]), 0.0)
    p_own = jnp.exp(s_own - m)
    den = jnp.sum(jnp.sum(p, axis=0), axis=1, keepdims=True) + p_own
    inv = 1.0 / den
    p_ref[0] = p * inv[None]
    pown_ref[0] = jnp.broadcast_to(p_own * inv, (n_heads, LANES))
    idx_ref[0] = idx_out


def _sample_value_kernel(pt_ref, sel_ref, p_ref, pown_ref, vn_ref, *rest, n_sel, ppb, n_heads):
    v_refs = rest[:-1]
    o_ref = rest[-1]
    d = pl.program_id(0)
    page = p_ref.shape[-1]
    head_dim = o_ref.shape[-1]
    lane = lax.broadcasted_iota(jnp.int32, (head_dim, page), 1)
    cols = jnp.zeros((head_dim, page), F32)
    for h in range(n_heads):
        acc = None
        for r in range(n_sel):
            s = h * n_sel + r
            blk = sel_ref[d * (n_heads * n_sel) + s]
            for u in range(ppb):
                pp = p_ref[0, blk * ppb + u, h:h + 1, :]
                term = v_refs[s * ppb + u][0, 0, 0] * pp
                acc = term if acc is None else acc + term
        cols = jnp.where(lane == h, jnp.sum(acc, axis=1, keepdims=True), cols)
    if head_dim < page:
        cols = jnp.concatenate([cols, jnp.zeros((page - head_dim, page), F32)], axis=0)
    out = cols.T[:n_heads, :head_dim]
    o_ref[0] = out + pown_ref[0][:, 0:1] * vn_ref[0]


def _moba_sample(s_all, q, k_new, v_new, vt, pt_flat, layer, *, n_heads, head_dim):
    db, da = q.shape
    n_pages, page = s_all.shape[1], s_all.shape[3]
    ppb = MOBA_BLOCK // page
    assert ppb >= 1 and ppb & (ppb - 1) == 0 and MOBA_BLOCK % page == 0
    assert (n_pages * page) % MOBA_BLOCK == 0
    n_blk = n_pages // ppb
    n_sel = min(MOBA_TOPK, n_blk)
    assert n_sel > 0 and n_sel <= LANES
    q3, kn3 = (a.reshape(db, 1, da) for a in (q, k_new))
    vn3 = v_new.reshape(db, n_heads, head_dim)

    row3 = pl.BlockSpec((1, 1, da), lambda d: (d, 0, 0))
    head_spec = pl.BlockSpec((1, n_heads, LANES), lambda d: (d, 0, 0))
    s_spec = pl.BlockSpec((1, n_pages, n_heads, page), lambda d: (d, 0, 0, 0))
    p_all, p_own, idx = pl.pallas_call(
        functools.partial(_sample_select_kernel, ppb=ppb, n_heads=n_heads, head_dim=head_dim,
                          n_sel=n_sel),
        grid=(db,),
        in_specs=[s_spec, row3, row3],
        out_specs=[s_spec, head_spec, head_spec],
        out_shape=[jax.ShapeDtypeStruct(s_all.shape, F32),
                   jax.ShapeDtypeStruct((db, n_heads, LANES), F32),
                   jax.ShapeDtypeStruct((db, n_heads, LANES), jnp.int32)],
        compiler_params=_cparams(("arbitrary",)),
        name="sample_select",
    )(s_all, q3, kn3)

    sel_flat = idx[:, :, :n_sel].reshape(-1)
    n_fetch = n_heads * n_sel * ppb
    head3 = pl.BlockSpec((1, n_heads, head_dim), lambda d, pt, sel: (d, 0, 0))

    def v_slab(d, pt, sel, s, u):
        d = jnp.minimum(d, db - 1)
        blk = jnp.clip(sel[d * (n_heads * n_sel) + s], 0, n_blk - 1)
        return (layer, pt[d * n_pages + blk * ppb + u], s // n_sel, 0, 0)

    v_specs = [pl.BlockSpec((1, 1, 1, head_dim, page),
                            functools.partial(v_slab, s=f // ppb, u=f % ppb))
               for f in range(n_fetch)]
    out = pl.pallas_call(
        functools.partial(_sample_value_kernel, n_sel=n_sel, ppb=ppb, n_heads=n_heads),
        grid_spec=pltpu.PrefetchScalarGridSpec(
            num_scalar_prefetch=2,
            grid=(db,),
            in_specs=[pl.BlockSpec((1, n_pages, n_heads, page), lambda d, pt, sel: (d, 0, 0, 0)),
                      pl.BlockSpec((1, n_heads, LANES), lambda d, pt, sel: (d, 0, 0)),
                      head3] + v_specs,
            out_specs=head3),
        out_shape=jax.ShapeDtypeStruct((db, n_heads, head_dim), F32),
        compiler_params=_cparams(("arbitrary",)),
        name="sample_value",
    )(pt_flat, sel_flat, p_all, p_own, vn3, *([vt] * n_fetch))
    return out.reshape(db, da)


def _block_diag(w):
    n, wi, wo = w.shape
    eye = jnp.eye(n, dtype=w.dtype)
    return (eye[:, None, :, None] * w[:, :, None, :]).reshape(n * wi, n * wo)


def _pick_tile(n, cap):
    t = min(n, cap)
    while n % t:
        t //= 2
    return t


def kernel(x_prompt, x_sample, cache_k, cache_v, state_rglru_h, state_conv, page_table, rms_g,
           q_norm_g, k_norm_g, w_in, conv_w, conv_b, w_a, b_a, w_x, b_x, lru_lambda, w_out):
    depth = w_in.shape[0]
    batch, seq, d = x_prompt.shape
    db, dec_seq, _ = x_sample.shape
    _, n_pool, page, n_heads, head_dim = cache_k.shape
    da = n_heads * head_dim
    dr = state_rglru_h.shape[-1]
    n_pages = page_table.shape[1]
    past_len = n_pages * page
    assert dec_seq == 1, "sample group is one new token per sequence"
    assert w_in.shape[2] == 4 * da + 2 * dr and da % LANES == 0 and dr % LANES == 0

    tm = _pick_tile(seq, 512)
    tms = _pick_tile(db, 512)
    pos_p = jnp.arange(seq)
    pos_s = jnp.full((db,), past_len, jnp.int32)
    bd = _block_diag(jnp.ones((LANES // head_dim, head_dim, head_dim), BF16))
    cache_kt = jnp.transpose(cache_k, (0, 1, 3, 4, 2))
    cache_vt = jnp.transpose(cache_v, (0, 1, 3, 4, 2))
    pt_flat = page_table.reshape(-1).astype(jnp.int32)

    yp = x_prompt.reshape(batch * seq, d)
    ys = x_sample.reshape(db, d)
    outs = {n: [] for n in ("kp", "vp", "hp", "cp", "ks", "vs", "hs", "cs")}
    for l in range(depth):
        w_in_bf = w_in[l].astype(BF16)
        w_out_bf = w_out[l].astype(BF16)
        wax = jnp.concatenate([_block_diag(w_a[l]), _block_diag(w_x[l])], axis=1).astype(BF16)
        proj = functools.partial(_proj_in, rms_g=rms_g[l], gq=q_norm_g[l], gk=k_norm_g[l],
                                 w_bf=w_in_bf, bd=bd, da=da, dr=dr, head_dim=head_dim)
        rnn_w = (conv_w[l], conv_b[l], wax, b_a[l], b_x[l], lru_lambda[l])

        q, k, v, ga, xr, gr, ka, vb, km = proj(yp, pos_p, tm=tm, with_prompt_outs=True)
        q_s, k_s, v_s, ga_s, xr_s, gr_s = proj(ys, pos_s, tm=tms, with_prompt_outs=False)

        km = km[:, :tm // MOBA_BLOCK, :].reshape(batch, seq // MOBA_BLOCK, da)
        attn, s_all = _moba_prompt(q, ka, vb, km, q_s, cache_kt, pt_flat, l, batch=batch, seq=seq,
                                   n_pages=n_pages, head_dim=head_dim)
        rnn, h_last, cbuf = _rnn_prompt(xr, gr, *rnn_w, batch=batch, seq=seq, tr=tm)
        yp = _proj_out(yp, attn, ga, rnn, w_out_bf, tm=tm)
        to_out = lambda t: t.reshape(batch, n_heads, head_dim, seq).transpose(0, 3, 1, 2)
        outs["kp"].append(to_out(k))
        outs["vp"].append(to_out(v))
        outs["hp"].append(h_last.reshape(batch, dr))
        outs["cp"].append(cbuf)

        q, k, v, ga, xr, gr = q_s, k_s, v_s, ga_s, xr_s, gr_s
        attn = _moba_sample(s_all, q, k, v, cache_vt, pt_flat, l, n_heads=n_heads, head_dim=head_dim)
        cs = state_conv[l].reshape(db, -1)
        rnn, h_last, cs_new = _rnn_sample(xr, gr, cs, state_rglru_h[l], *rnn_w)
        ys = _proj_out(ys, attn, ga, rnn, w_out_bf, tm=tms)
        outs["ks"].append(k.reshape(db, 1, n_heads, head_dim))
        outs["vs"].append(v.reshape(db, 1, n_heads, head_dim))
        outs["hs"].append(h_last)
        outs["cs"].append(cs_new.reshape(db, -1, dr))

    st = lambda n: jnp.stack(outs[n])
    return (yp.reshape(batch, seq, d), ys.reshape(db, 1, d), st("kp"), st("vp"), st("hp"),
            st("cp"), st("ks"), st("vs"), st("hs"), st("cs"))
```

```python
import functools

import jax
import jax.numpy as jnp
from jax import lax
from jax.experimental import pallas as pl
from jax.experimental.pallas import tpu as pltpu

ROPE_THETA = 500000.0
ROPE_FRACTION = 4
MOBA_BLOCK = 256
MOBA_TOPK = 3
LRU_C = 8.0
EPS = 1e-6
NEG = -1e30
LOG2E = 1.4426950408889634
SELECT_SEQS_PER_STEP = 4

LANES = 128
SUBLANES = 8
VMEM_LIMIT_BYTES = 56 * 1024 * 1024

F32 = jnp.float32
BF16 = jnp.bfloat16
NT_DIMS = (((1,), (1,)), ((), ()))


def _cparams(sem):
    return pltpu.CompilerParams(dimension_semantics=sem, vmem_limit_bytes=VMEM_LIMIT_BYTES)


def _sigmoid(x):
    return 0.5 * jnp.tanh(0.5 * x) + 0.5


def _silu(x):
    return x * _sigmoid(x)


def _proj_in_kernel(x_ref, g_ref, w_ref, gq_ref, gk_ref, bd_ref, c_ref, s1_ref, s2_ref,
                    q_ref, k_ref, v_ref, ga_ref, xr_ref, gr_ref, *rest,
                    da, dr, head_dim, pos_tiles, with_prompt_outs):
    x = x_ref[...]
    ms = jnp.mean(x * x, axis=-1, keepdims=True)
    h = (x * lax.rsqrt(ms + EPS) * g_ref[...]).astype(BF16)

    cosf, sin_up, sin_dn = c_ref[...], s1_ref[...], s2_ref[...]
    bd = bd_ref[...]
    half = head_dim // ROPE_FRACTION // 2

    def proj(start, width):
        return jnp.dot(h, w_ref[:, start:start + width], preferred_element_type=F32)

    def head_norm_rope(t, gain):
        sq = t * t
        hi = sq.astype(BF16)
        lo = (sq - hi.astype(F32)).astype(BF16)
        cols = []
        for c in range(da // LANES):
            grp = slice(c * LANES, (c + 1) * LANES)
            ssum = (jnp.dot(hi[:, grp], bd, preferred_element_type=F32)
                    + jnp.dot(lo[:, grp], bd, preferred_element_type=F32))
            tc = t[:, grp] * lax.rsqrt(ssum * (1.0 / head_dim) + EPS) * gain[:, grp]
            up = pltpu.roll(tc, LANES - half, 1)
            dn = pltpu.roll(tc, half, 1)
            cols.append(tc * cosf + up * sin_up + dn * sin_dn)
        return jnp.concatenate(cols, axis=1)

    q_ref[...] = head_norm_rope(proj(0, da), gq_ref[...])
    k = head_norm_rope(proj(da, da), gk_ref[...])
    v = proj(2 * da, da)
    if with_prompt_outs:
        k_ref[0] = k.T
        v_ref[0] = v.T
    else:
        k_ref[...] = k
        v_ref[...] = v
    ga_ref[...] = proj(3 * da, da)
    xr_ref[...] = proj(4 * da, dr)
    gr_ref[...] = proj(4 * da + dr, dr)

    if with_prompt_outs:
        ka_ref, vb_ref, km_ref = rest
        tm = k.shape[0]
        vb_ref[...] = v.astype(BF16)
        lane = lax.broadcasted_iota(jnp.int32, (tm, LANES), 1)
        row_blk = lax.shift_right_logical(lax.broadcasted_iota(jnp.int32, (tm, LANES), 0),
                                          MOBA_BLOCK.bit_length() - 1)
        blk_id = (pl.program_id(0) % pos_tiles) * (tm // MOBA_BLOCK) + row_blk
        onehot = jnp.where(lane == blk_id, 1.0, 0.0).astype(BF16)
        kb = k.astype(BF16)
        ka_ref[...] = jnp.concatenate(
            [piece for c in range(da // LANES) for piece in (kb[:, c * LANES:(c + 1) * LANES], onehot)],
            axis=1)
        rows = [jnp.mean(k[b * MOBA_BLOCK:(b + 1) * MOBA_BLOCK], axis=0, keepdims=True)
                for b in range(tm // MOBA_BLOCK)]
        rows.append(jnp.zeros((SUBLANES - len(rows), da), F32))
        km_ref[0] = jnp.concatenate(rows, axis=0)


def _rope_tables(pos, head_dim):
    rope_dim = head_dim // ROPE_FRACTION
    half = rope_dim // 2
    inv = ROPE_THETA ** (-jnp.arange(half, dtype=F32) * 2.0 / rope_dim)
    ang = pos.astype(F32)[:, None] * inv
    cos, sin = jnp.cos(ang), jnp.sin(ang)
    n = pos.shape[0]
    ones = jnp.ones((n, head_dim - rope_dim), F32)
    zeros = jnp.zeros((n, head_dim - rope_dim), F32)
    zh = jnp.zeros((n, half), F32)
    reps = LANES // head_dim
    cosf = jnp.tile(jnp.concatenate([cos, cos, ones], axis=1), (1, reps))
    sin_up = jnp.tile(jnp.concatenate([-sin, zh, zeros], axis=1), (1, reps))
    sin_dn = jnp.tile(jnp.concatenate([zh, sin, zeros], axis=1), (1, reps))
    return cosf, sin_up, sin_dn


def _proj_in(x2d, pos, rms_g, gq, gk, w_bf, bd, *, da, dr, head_dim, tm, with_prompt_outs):
    rows, d = x2d.shape
    n_pos = pos.shape[0]
    assert rows % tm == 0 and n_pos % tm == 0
    pos_tiles = n_pos // tm
    cosf, sin_up, sin_dn = _rope_tables(pos, head_dim)
    reps = da // head_dim

    row_spec = lambda w: pl.BlockSpec((tm, w), lambda i: (i, 0))
    full = lambda a: pl.BlockSpec(a.shape, lambda i: (0,) * a.ndim)
    tab_spec = pl.BlockSpec((tm, LANES), lambda i: (i % pos_tiles, 0))

    g2 = rms_g.reshape(1, d)
    gq2 = jnp.tile(gq, reps).reshape(1, da)
    gk2 = jnp.tile(gk, reps).reshape(1, da)

    out_shape = [jax.ShapeDtypeStruct((rows, da), F32)] * 4 + [jax.ShapeDtypeStruct((rows, dr), F32)] * 2
    out_specs = [row_spec(da)] * 4 + [row_spec(dr)] * 2
    if with_prompt_outs:
        out_shape[1:3] = [jax.ShapeDtypeStruct((rows // n_pos, da, n_pos), F32)] * 2
        out_specs[1:3] = [pl.BlockSpec((1, da, tm), lambda i: (i // pos_tiles, 0, i % pos_tiles))] * 2
        assert tm % MOBA_BLOCK == 0 and tm // MOBA_BLOCK <= SUBLANES
        assert n_pos // MOBA_BLOCK <= LANES and MOBA_BLOCK & (MOBA_BLOCK - 1) == 0
        out_shape += [jax.ShapeDtypeStruct((rows, 2 * da), BF16)]
        out_shape += [jax.ShapeDtypeStruct((rows, da), BF16)]
        out_shape += [jax.ShapeDtypeStruct((rows // tm, SUBLANES, da), F32)]
        out_specs += [row_spec(2 * da), row_spec(da)]
        out_specs += [pl.BlockSpec((1, SUBLANES, da), lambda i: (i, 0, 0))]

    kern = functools.partial(_proj_in_kernel, da=da, dr=dr, head_dim=head_dim, pos_tiles=pos_tiles,
                             with_prompt_outs=with_prompt_outs)
    return pl.pallas_call(
        kern,
        grid=(rows // tm,),
        in_specs=[row_spec(d), full(g2), full(w_bf), full(gq2), full(gk2), full(bd),
                  tab_spec, tab_spec, tab_spec],
        out_specs=out_specs,
        out_shape=out_shape,
        compiler_params=_cparams(("arbitrary",)),
        name="proj_in",
    )(x2d, g2, w_bf, gq2, gk2, bd, cosf, sin_up, sin_dn)


def _proj_out_kernel(x_ref, *refs, da, half_tiles):
    if half_tiles:
        lo_ref, hi_ref, ga_ref, rnn_ref, w_ref, y_ref = refs
        in_low_half = pl.program_id(0) % (2 * half_tiles) < half_tiles
        attn = jnp.where(in_low_half, lo_ref[...], hi_ref[...])
    else:
        attn_ref, ga_ref, rnn_ref, w_ref, y_ref = refs
        attn = attn_ref[...]
    a = (attn * _silu(ga_ref[...])).astype(BF16)
    r = rnn_ref[...].astype(BF16)
    y = jnp.dot(a, w_ref[0:da, :], preferred_element_type=F32)
    y = y + jnp.dot(r, w_ref[da:, :], preferred_element_type=F32)
    y_ref[...] = x_ref[...] + y


def _proj_out(x2d, attn, ga, rnn, w_bf, *, tm, tiles_per_seq=None):
    rows, d = x2d.shape
    da, dr = ga.shape[1], rnn.shape[1]
    assert rows % tm == 0
    row_spec = lambda w: pl.BlockSpec((tm, w), lambda i: (i, 0))
    if isinstance(attn, tuple):
        assert tiles_per_seq % 2 == 0
        half_tiles = tiles_per_seq // 2
        seq_base = lambda i: (i // tiles_per_seq) * half_tiles
        attn_specs = [
            pl.BlockSpec((tm, da), lambda i: (seq_base(i) + jnp.minimum(i % tiles_per_seq,
                                                                        half_tiles - 1), 0)),
            pl.BlockSpec((tm, da), lambda i: (seq_base(i) + jnp.maximum(i % tiles_per_seq
                                                                        - half_tiles, 0), 0))]
        attn_args = attn
    else:
        half_tiles, attn_specs, attn_args = 0, [row_spec(da)], (attn,)
    return pl.pallas_call(
        functools.partial(_proj_out_kernel, da=da, half_tiles=half_tiles),
        grid=(rows // tm,),
        in_specs=[row_spec(d)] + attn_specs + [row_spec(da), row_spec(dr),
                                               pl.BlockSpec(w_bf.shape, lambda i: (0, 0))],
        out_specs=row_spec(d),
        out_shape=jax.ShapeDtypeStruct((rows, d), F32),
        compiler_params=_cparams(("arbitrary",)),
        name="proj_out",
    )(x2d, *attn_args, ga, rnn, w_bf)


def _lru_coeffs(xc, wax_ref, ba_ref, bx_ref, lam_ref, dr):
    pre = jnp.dot(xc.astype(BF16), wax_ref[...], preferred_element_type=F32)
    r = _sigmoid(pre[:, :dr] + ba_ref[...])
    gi = _sigmoid(pre[:, dr:] + bx_ref[...])
    z = -lam_ref[...]
    softplus = jnp.maximum(z, 0.0) + jnp.log1p(jnp.exp(-jnp.abs(z)))
    log_a = -LRU_C * r * softplus
    a = jnp.exp(log_a)
    gap = 1.0 - a * a
    mult = jnp.where(gap > 0.0, gap * lax.rsqrt(gap), 0.0)
    return a, mult * (gi * xc)


def _rnn_prompt_kernel(xr_ref, gr_ref, cw_ref, cb_ref, wax_ref, ba_ref, bx_ref, lam_ref,
                       out_ref, hl_ref, cbuf_ref, xe_ref, hc_ref, *, tr, dr, conv_w):
    t = pl.program_id(1)
    pad = SUBLANES

    @pl.when(t == 0)
    def _():
        xe_ref[0:pad, :] = jnp.zeros((pad, dr), F32)
        hc_ref[...] = jnp.zeros_like(hc_ref)

    x = xr_ref[...]
    xe_ref[pad:pad + tr, :] = x
    y = cb_ref[...] + cw_ref[conv_w - 1:conv_w, :] * x
    for j in range(conv_w - 1):
        back = conv_w - 1 - j
        y = y + cw_ref[j:j + 1, :] * xe_ref[pad - back:pad - back + tr, :]
    cbuf_ref[0] = xe_ref[pad + tr - (conv_w - 1):pad + tr, :]
    xe_ref[0:pad, :] = xe_ref[tr:tr + pad, :]

    a, b = _lru_coeffs(y, wax_ref, ba_ref, bx_ref, lam_ref, dr)

    grp = SUBLANES
    a = a.reshape(tr // grp, grp, dr)
    b = b.reshape(tr // grp, grp, dr)
    sub = lax.broadcasted_iota(jnp.int32, a.shape, 1)
    s = 1
    while s < grp:
        keep = sub >= s
        a_sh = pltpu.roll(a, s, 1)
        b_sh = pltpu.roll(b, s, 1)
        b = jnp.where(keep, a * b_sh + b, b)
        a = jnp.where(keep, a * a_sh, a)
        s *= 2
    h_prev = hc_ref[...]
    h_groups = []
    for g in range(tr // grp):
        h_g = a[g] * h_prev + b[g]
        h_groups.append(h_g)
        h_prev = h_g[grp - 1:grp, :]
    h = jnp.concatenate(h_groups, axis=0)
    h_last = h_prev
    hc_ref[...] = h_last
    hl_ref[0] = h_last
    out_ref[...] = h * _silu(gr_ref[...])


def _rnn_prompt(xr, gr, conv_w, conv_b, wax, b_a, b_x, lam, *, batch, seq, tr):
    rows, dr = xr.shape
    cw = conv_w.shape[0]
    assert seq % tr == 0 and rows == batch * seq and cw - 1 <= SUBLANES
    nt = seq // tr
    row_spec = pl.BlockSpec((tr, dr), lambda b, t: (b * nt + t, 0))
    full = lambda a: pl.BlockSpec(a.shape, lambda b, t: (0,) * a.ndim)
    vec = lambda a: a.reshape(1, dr)
    args = (xr, gr, conv_w, vec(conv_b), wax, vec(b_a), vec(b_x), vec(lam))
    return pl.pallas_call(
        functools.partial(_rnn_prompt_kernel, tr=tr, dr=dr, conv_w=cw),
        grid=(batch, nt),
        in_specs=[row_spec, row_spec] + [full(a) for a in args[2:]],
        out_specs=[row_spec,
                   pl.BlockSpec((1, 1, dr), lambda b, t: (b, 0, 0)),
                   pl.BlockSpec((1, cw - 1, dr), lambda b, t: (b, 0, 0))],
        out_shape=[jax.ShapeDtypeStruct((rows, dr), F32),
                   jax.ShapeDtypeStruct((batch, 1, dr), F32),
                   jax.ShapeDtypeStruct((batch, cw - 1, dr), F32)],
        scratch_shapes=[pltpu.VMEM((SUBLANES + tr, dr), F32), pltpu.VMEM((1, dr), F32)],
        compiler_params=_cparams(("arbitrary", "arbitrary")),
        name="rnn_prompt",
    )(*args)


def _rnn_sample_kernel(xr_ref, gr_ref, cs_ref, h0_ref, cw_ref, cb_ref, wax_ref, ba_ref, bx_ref,
                       lam_ref, out_ref, hl_ref, cs_out_ref, *, dr, conv_w):
    x = xr_ref[...]
    y = cb_ref[...] + cw_ref[conv_w - 1:conv_w, :] * x
    for j in range(conv_w - 1):
        y = y + cw_ref[j:j + 1, :] * cs_ref[:, j * dr:(j + 1) * dr]
    a, b = _lru_coeffs(y, wax_ref, ba_ref, bx_ref, lam_ref, dr)
    h = a * h0_ref[...] + b
    hl_ref[...] = h
    out_ref[...] = h * _silu(gr_ref[...])
    if conv_w > 2:
        cs_out_ref[:, 0:(conv_w - 2) * dr] = cs_ref[:, dr:(conv_w - 1) * dr]
    cs_out_ref[:, (conv_w - 2) * dr:] = x


def _rnn_sample(xr, gr, cs, h0, conv_w, conv_b, wax, b_a, b_x, lam):
    rows, dr = xr.shape
    cw = conv_w.shape[0]
    vec = lambda a: a.reshape(1, dr)
    args = (xr, gr, cs, h0, conv_w, vec(conv_b), wax, vec(b_a), vec(b_x), vec(lam))
    full = lambda a: pl.BlockSpec(a.shape, lambda i: (0,) * a.ndim)
    return pl.pallas_call(
        functools.partial(_rnn_sample_kernel, dr=dr, conv_w=cw),
        grid=(1,),
        in_specs=[full(a) for a in args],
        out_specs=[full(xr), full(xr), full(cs)],
        out_shape=[jax.ShapeDtypeStruct((rows, dr), F32), jax.ShapeDtypeStruct((rows, dr), F32),
                   jax.ShapeDtypeStruct(cs.shape, F32)],
        compiler_params=_cparams(("arbitrary",)),
        name="rnn_sample",
    )(*args)


def _first_max_pick(s, idx, axis, sentinel):
    m = jnp.max(s, axis=axis, keepdims=True)
    first = jnp.min(jnp.where((s == m) & (m > -jnp.inf), idx, sentinel), axis=axis, keepdims=True)
    return idx == first, first


def _bf16_parts(x):
    hi = x.astype(BF16)
    return hi, (x - hi.astype(F32)).astype(BF16)


def _moba_kernel(pt_ref, q_lo_ref, q_hi_ref, ka_ref, vb_ref, km_ref, qcol_ref, *rest,
                 head_dim, n_blocks, pages_per_step):
    page_refs = rest[:pages_per_step]
    o_lo_ref, o_hi_ref, s_page_ref, s_sc, qa_sc, mx_sc, acc_sc = rest[pages_per_step:]

    i = pl.program_id(2)
    blk = MOBA_BLOCK
    qscale = head_dim ** -0.5 * LOG2E
    n_past = n_blocks - 1
    lane = lax.broadcasted_iota(jnp.int32, (blk, LANES), 1)
    n_hh = LANES // head_dim
    future = (lax.broadcasted_iota(jnp.int32, (blk, blk), 1)
              > lax.broadcasted_iota(jnp.int32, (blk, blk), 0))
    brow = lax.broadcasted_iota(jnp.int32, (n_blocks, blk), 0)
    km = km_ref[0]
    q_refs = (q_lo_ref, q_hi_ref)
    qblks = (i, n_blocks - 1 - i)

    def past_tile(j):
        second = j >= i
        start = pl.multiple_of(jnp.where(second, j - i, j) * blk, blk)
        return second.astype(jnp.int32), start

    km_parts = _bf16_parts(km)
    for hh in range(n_hh):
        in_head = (lane >= hh * head_dim) & (lane < (hh + 1) * head_dim)
        for w in range(2):
            qm = jnp.where(in_head, q_refs[w][...], 0.0)
            q_parts = _bf16_parts(qm)
            sb = sum(lax.dot_general(km_parts[a], q_parts[b], NT_DIMS, preferred_element_type=F32)
                     for a, b in ((0, 0), (0, 1), (1, 0)))
            sb = jnp.where(brow < qblks[w], sb, -jnp.inf)
            sel = brow == qblks[w]
            for _ in range(min(MOBA_TOPK, n_blocks)):
                pick, _ = _first_max_pick(sb, brow, 0, n_blocks)
                sel = sel | pick
                sb = jnp.where(pick, -jnp.inf, sb)
            bias_t = jnp.where(sel, 0.0, NEG)
            bias_t = jnp.concatenate([bias_t, jnp.full((LANES - n_blocks, blk), NEG, F32)], axis=0)
            qa_sc[w, hh * blk:(hh + 1) * blk, :] = jnp.concatenate(
                [(qm * qscale).astype(BF16), bias_t.T.astype(BF16)], axis=1)
    mx_sc[...] = jnp.full(mx_sc.shape, NEG, F32)
    acc_sc[...] = jnp.zeros(acc_sc.shape, F32)
    ones = jnp.ones((blk, LANES), BF16)

    def logits(which, start, slot, own):
        ka = ka_ref[pl.ds(start, blk), :]
        for hh in range(n_hh):
            rows = pl.ds(hh * blk, blk)
            s = lax.dot_general(qa_sc[which, rows, :], ka, NT_DIMS, preferred_element_type=F32)
            if own:
                s = jnp.where(future, NEG, s)
            s_sc[slot, rows, :] = s
            mx_sc[which, rows, :] = jnp.maximum(mx_sc[which, rows, :],
                                                jnp.maximum(s[:, :LANES], s[:, LANES:]))

    def weighted_values(which, start, slot):
        va = jnp.concatenate([vb_ref[pl.ds(start, blk), :], ones], axis=1)
        for hh in range(n_hh):
            rows = pl.ds(hh * blk, blk)
            m = mx_sc[which, rows, :]
            s = s_sc[slot, rows, :]
            p = jnp.concatenate([jnp.exp2(s[:, :LANES] - m), jnp.exp2(s[:, LANES:] - m)], axis=1)
            acc_sc[which, rows, :] = acc_sc[which, rows, :] + jnp.dot(
                p.astype(BF16), va, preferred_element_type=F32)

    own_tiles = [(w, pl.multiple_of(qblks[w] * blk, blk), n_past + w) for w in range(2)]

    half_pages = pages_per_step // 2

    def tile_pass(tile_fn, pages, first_page):
        @pl.when(i >= 0)
        def _():
            for j in range(n_past):
                which, start = past_tile(j)
                tile_fn(which, start, j, False)
            for which, start, slot in own_tiles:
                tile_fn(which, start, slot, True)
            _page_logits(pages, qcol_ref, s_page_ref, first_page, head_dim ** -0.5)

    tile_pass(logits, page_refs[:half_pages], 0)
    for w in range(2):
        mx_sc[w] = jnp.broadcast_to(jnp.max(mx_sc[w], axis=1, keepdims=True), mx_sc.shape[1:])
    tile_pass(lambda which, start, slot, own: weighted_values(which, start, slot),
              page_refs[half_pages:], half_pages)

    for w, o_ref in enumerate((o_lo_ref, o_hi_ref)):
        out = None
        for hh in range(n_hh - 1, -1, -1):
            acc = acc_sc[w, hh * blk:(hh + 1) * blk, :]
            o_h = acc[:, :LANES] / acc[:, LANES:]
            out = o_h if out is None else jnp.where(lane < (hh + 1) * head_dim, o_h, out)
        o_ref[...] = out


def _moba_prompt(q, ka, vb, km, q_s, kt, pt_flat, layer, *, batch, seq, n_pages, head_dim):
    rows, da = q.shape
    assert LANES % head_dim == 0 and seq % MOBA_BLOCK == 0 and da % LANES == 0
    nb = seq // MOBA_BLOCK
    assert nb % SUBLANES == 0 and nb <= LANES
    blk = MOBA_BLOCK
    n_hh = LANES // head_dim
    half = nb // 2
    n_hp = da // LANES

    db = q_s.shape[0]
    n_heads, page = kt.shape[2], kt.shape[4]
    assert page == LANES
    n_steps = batch * n_hp * half
    total_pages = db * n_pages
    assert total_pages % n_steps == 0
    pps = total_pages // n_steps
    assert n_pages % pps == 0
    steps_per_seq = n_pages // pps
    qcol = jnp.broadcast_to(q_s.reshape(db, n_heads, head_dim, 1), (db, n_heads, head_dim, page))

    step = lambda b, hp, i: (b * n_hp + hp) * half + i
    q_spec = lambda f: pl.BlockSpec((blk, LANES), lambda b, hp, i, pt: (b * nb + f(i), hp))
    o_spec = lambda f: pl.BlockSpec((blk, LANES), lambda b, hp, i, pt: (b * half + f(i), hp))
    page_specs = [
        pl.BlockSpec((1, 1, n_heads, head_dim, page),
                     lambda b, hp, i, pt, ii=ii:
                     (layer, pt[jnp.minimum(step(b, hp, i) * pps + ii, total_pages - 1)], 0, 0, 0))
        for ii in range(pps)]
    half_shape = jax.ShapeDtypeStruct((rows // 2, da), F32)
    lo, hi, s_all = pl.pallas_call(
        functools.partial(_moba_kernel, head_dim=head_dim, n_blocks=nb, pages_per_step=pps),
        grid_spec=pltpu.PrefetchScalarGridSpec(
            num_scalar_prefetch=1,
            grid=(batch, n_hp, half),
            in_specs=[q_spec(lambda i: i), q_spec(lambda i: nb - 1 - i),
                      pl.BlockSpec((seq, 2 * LANES), lambda b, hp, i, pt: (b, hp)),
                      pl.BlockSpec((seq, LANES), lambda b, hp, i, pt: (b, hp)),
                      pl.BlockSpec((1, nb, LANES), lambda b, hp, i, pt: (b, 0, hp)),
                      pl.BlockSpec((1, n_heads, head_dim, page),
                                   lambda b, hp, i, pt: (step(b, hp, i) // steps_per_seq, 0, 0, 0))]
            + page_specs,
            out_specs=[o_spec(lambda i: i), o_spec(lambda i: half - 1 - i),
                       pl.BlockSpec((1, pps, n_heads, page),
                                    lambda b, hp, i, pt: (step(b, hp, i), 0, 0, 0))],
            scratch_shapes=[pltpu.VMEM((nb + 1, n_hh * blk, blk), F32),
                            pltpu.VMEM((2, n_hh * blk, 2 * LANES), BF16),
                            pltpu.VMEM((2, n_hh * blk, LANES), F32),
                            pltpu.VMEM((2, n_hh * blk, 2 * LANES), F32)]),
        out_shape=[half_shape, half_shape,
                   jax.ShapeDtypeStruct((n_steps, pps, n_heads, page), F32)],
        compiler_params=_cparams(("arbitrary", "arbitrary", "arbitrary")),
        name="moba_prompt",
    )(pt_flat, q, q, ka, vb, km, qcol, *([kt] * pps))
    return (lo, hi), s_all.reshape(db, n_pages, n_heads, page)


def _head_diag(n_heads, width, head_dim):
    sub = lax.broadcasted_iota(jnp.int32, (n_heads, width), 0)
    lane = lax.broadcasted_iota(jnp.int32, (n_heads, width), 1)
    return (lane >= sub * head_dim) & (lane < (sub + 1) * head_dim)


def _page_logits(k_refs, qcol_ref, s_ref, first, scale):
    qcol = qcol_ref[0]
    for ii, k_ref in enumerate(k_refs):
        s_ref[0, first + ii] = jnp.sum(k_ref[0, 0] * qcol, axis=1) * scale


def _sample_select_kernel(s_ref, q_ref, kn_ref, p_ref, pown_ref, idx_ref, **static):
    for dd in range(s_ref.shape[0]):
        _sample_select_one(dd, s_ref, q_ref, kn_ref, p_ref, pown_ref, idx_ref, **static)


def _sample_select_one(dd, s_ref, q_ref, kn_ref, p_ref, pown_ref, idx_ref, *, ppb, n_heads, head_dim,
                       n_sel):
    da = n_heads * head_dim
    scale = head_dim ** -0.5
    s = s_ref[dd]
    n_blk = s.shape[0] // ppb
    tok = jnp.sum(s, axis=2, keepdims=True)
    sb = jnp.sum(tok.reshape(n_blk, ppb, n_heads, 1), axis=1)
    bidx = lax.broadcasted_iota(jnp.int32, sb.shape, 0)
    pblk = lax.shift_right_logical(
        lax.broadcasted_iota(jnp.int32, s.shape, 0), ppb.bit_length() - 1)
    lane = lax.broadcasted_iota(jnp.int32, (n_heads, LANES), 1)
    sel = jnp.zeros(s.shape, jnp.bool_)
    idx_out = jnp.zeros((n_heads, LANES), jnp.int32)
    for r in range(n_sel):
        pick, first = _first_max_pick(sb, bidx, 0, n_blk)
        sb = jnp.where(pick, -jnp.inf, sb)
        sel = sel | (pblk == first)
        idx_out = jnp.where(lane == r, first[0], idx_out)
    diag = _head_diag(n_heads, da, head_dim)
    qk_new = jnp.broadcast_to(q_ref[dd] * kn_ref[dd], (n_heads, da))
    s_own = jnp.sum(jnp.where(diag, qk_new, 0.0), axis=1, keepdims=True) * scale
    sm = jnp.where(sel, s, -jnp.inf)
    m = jnp.max(jnp.max(sm, axis=0), axis=1, keepdims=True)
    m = jnp.maximum(m, s_own)
    p = jnp.where(sel, jnp.exp(s - m[None]), 0.0)
    p_own = jnp.exp(s_own - m)
    den = jnp.sum(jnp.sum(p, axis=0), axis=1, keepdims=True) + p_own
    inv = 1.0 / den
    p_ref[dd] = p * inv[None]
    pown_ref[dd] = jnp.broadcast_to(p_own * inv, (n_heads, LANES))
    idx_ref[dd] = idx_out


def _sample_value_kernel(pt_ref, sel_ref, p_ref, pown_ref, vn_ref, *rest, n_sel, ppb, n_heads):
    v_refs = rest[:-1]
    o_ref = rest[-1]
    d = pl.program_id(0)
    page = p_ref.shape[-1]
    head_dim = o_ref.shape[-1]
    lane = lax.broadcasted_iota(jnp.int32, (head_dim, page), 1)
    cols = jnp.zeros((head_dim, page), F32)
    for h in range(n_heads):
        acc = None
        for r in range(n_sel):
            s = h * n_sel + r
            blk = sel_ref[d * (n_heads * n_sel) + s]
            for u in range(ppb):
                pp = p_ref[0, blk * ppb + u, h:h + 1, :]
                term = v_refs[s * ppb + u][0, 0, 0] * pp
                acc = term if acc is None else acc + term
        cols = jnp.where(lane == h, jnp.sum(acc, axis=1, keepdims=True), cols)
    if head_dim < page:
        cols = jnp.concatenate([cols, jnp.zeros((page - head_dim, page), F32)], axis=0)
    out = cols.T[:n_heads, :head_dim]
    o_ref[0] = out + pown_ref[0][:, 0:1] * vn_ref[0]


def _moba_sample(s_all, q, k_new, v_new, vt, pt_flat, layer, *, n_heads, head_dim):
    db, da = q.shape
    n_pages, page = s_all.shape[1], s_all.shape[3]
    ppb = MOBA_BLOCK // page
    assert ppb >= 1 and ppb & (ppb - 1) == 0 and MOBA_BLOCK % page == 0
    assert (n_pages * page) % MOBA_BLOCK == 0
    n_blk = n_pages // ppb
    n_sel = min(MOBA_TOPK, n_blk)
    assert n_sel > 0 and n_sel <= LANES
    q3, kn3 = (a.reshape(db, 1, da) for a in (q, k_new))
    vn3 = v_new.reshape(db, n_heads, head_dim)

    nd = _pick_tile(db, SELECT_SEQS_PER_STEP)
    row3 = pl.BlockSpec((nd, 1, da), lambda d: (d, 0, 0))
    head_spec = pl.BlockSpec((nd, n_heads, LANES), lambda d: (d, 0, 0))
    s_spec = pl.BlockSpec((nd, n_pages, n_heads, page), lambda d: (d, 0, 0, 0))
    p_all, p_own, idx = pl.pallas_call(
        functools.partial(_sample_select_kernel, ppb=ppb, n_heads=n_heads, head_dim=head_dim,
                          n_sel=n_sel),
        grid=(db // nd,),
        in_specs=[s_spec, row3, row3],
        out_specs=[s_spec, head_spec, head_spec],
        out_shape=[jax.ShapeDtypeStruct(s_all.shape, F32),
                   jax.ShapeDtypeStruct((db, n_heads, LANES), F32),
                   jax.ShapeDtypeStruct((db, n_heads, LANES), jnp.int32)],
        compiler_params=_cparams(("arbitrary",)),
        name="sample_select",
    )(s_all, q3, kn3)

    sel_flat = idx[:, :, :n_sel].reshape(-1)
    n_fetch = n_heads * n_sel * ppb
    head3 = pl.BlockSpec((1, n_heads, head_dim), lambda d, pt, sel: (d, 0, 0))

    def v_slab(d, pt, sel, s, u):
        d = jnp.minimum(d, db - 1)
        blk = jnp.clip(sel[d * (n_heads * n_sel) + s], 0, n_blk - 1)
        return (layer, pt[d * n_pages + blk * ppb + u], s // n_sel, 0, 0)

    v_specs = [pl.BlockSpec((1, 1, 1, head_dim, page),
                            functools.partial(v_slab, s=f // ppb, u=f % ppb))
               for f in range(n_fetch)]
    out = pl.pallas_call(
        functools.partial(_sample_value_kernel, n_sel=n_sel, ppb=ppb, n_heads=n_heads),
        grid_spec=pltpu.PrefetchScalarGridSpec(
            num_scalar_prefetch=2,
            grid=(db,),
            in_specs=[pl.BlockSpec((1, n_pages, n_heads, page), lambda d, pt, sel: (d, 0, 0, 0)),
                      pl.BlockSpec((1, n_heads, LANES), lambda d, pt, sel: (d, 0, 0)),
                      head3] + v_specs,
            out_specs=head3),
        out_shape=jax.ShapeDtypeStruct((db, n_heads, head_dim), F32),
        compiler_params=_cparams(("arbitrary",)),
        name="sample_value",
    )(pt_flat, sel_flat, p_all, p_own, vn3, *([vt] * n_fetch))
    return out.reshape(db, da)


def _block_diag(w):
    n, wi, wo = w.shape
    eye = jnp.eye(n, dtype=w.dtype)
    return (eye[:, None, :, None] * w[:, :, None, :]).reshape(n * wi, n * wo)


def _pick_tile(n, cap):
    t = min(n, cap)
    while n % t:
        t //= 2
    return t


def kernel(x_prompt, x_sample, cache_k, cache_v, state_rglru_h, state_conv, page_table, rms_g,
           q_norm_g, k_norm_g, w_in, conv_w, conv_b, w_a, b_a, w_x, b_x, lru_lambda, w_out):
    depth = w_in.shape[0]
    batch, seq, d = x_prompt.shape
    db, dec_seq, _ = x_sample.shape
    _, n_pool, page, n_heads, head_dim = cache_k.shape
    da = n_heads * head_dim
    dr = state_rglru_h.shape[-1]
    n_pages = page_table.shape[1]
    past_len = n_pages * page
    assert dec_seq == 1, "sample group is one new token per sequence"
    assert w_in.shape[2] == 4 * da + 2 * dr and da % LANES == 0 and dr % LANES == 0

    tm = _pick_tile(seq, 512)
    tms = _pick_tile(db, 512)
    pos_p = jnp.arange(seq)
    pos_s = jnp.full((db,), past_len, jnp.int32)
    bd = _block_diag(jnp.ones((LANES // head_dim, head_dim, head_dim), BF16))
    cache_kt = jnp.transpose(cache_k, (0, 1, 3, 4, 2))
    cache_vt = jnp.transpose(cache_v, (0, 1, 3, 4, 2))
    pt_flat = page_table.reshape(-1).astype(jnp.int32)

    yp = x_prompt.reshape(batch * seq, d)
    ys = x_sample.reshape(db, d)
    outs = {n: [] for n in ("kp", "vp", "hp", "cp", "ks", "vs", "hs", "cs")}
    for l in range(depth):
        w_in_bf = w_in[l].astype(BF16)
        w_out_bf = w_out[l].astype(BF16)
        wax = jnp.concatenate([_block_diag(w_a[l]), _block_diag(w_x[l])], axis=1).astype(BF16)
        proj = functools.partial(_proj_in, rms_g=rms_g[l], gq=q_norm_g[l], gk=k_norm_g[l],
                                 w_bf=w_in_bf, bd=bd, da=da, dr=dr, head_dim=head_dim)
        rnn_w = (conv_w[l], conv_b[l], wax, b_a[l], b_x[l], lru_lambda[l])

        q, k, v, ga, xr, gr, ka, vb, km = proj(yp, pos_p, tm=tm, with_prompt_outs=True)
        q_s, k_s, v_s, ga_s, xr_s, gr_s = proj(ys, pos_s, tm=tms, with_prompt_outs=False)

        km = km[:, :tm // MOBA_BLOCK, :].reshape(batch, seq // MOBA_BLOCK, da)
        attn, s_all = _moba_prompt(q, ka, vb, km, q_s, cache_kt, pt_flat, l, batch=batch, seq=seq,
                                   n_pages=n_pages, head_dim=head_dim)
        rnn, h_last, cbuf = _rnn_prompt(xr, gr, *rnn_w, batch=batch, seq=seq, tr=tm)
        yp = _proj_out(yp, attn, ga, rnn, w_out_bf, tm=tm, tiles_per_seq=seq // tm)
        to_out = lambda t: t.reshape(batch, n_heads, head_dim, seq).transpose(0, 3, 1, 2)
        outs["kp"].append(to_out(k))
        outs["vp"].append(to_out(v))
        outs["hp"].append(h_last.reshape(batch, dr))
        outs["cp"].append(cbuf)

        q, k, v, ga, xr, gr = q_s, k_s, v_s, ga_s, xr_s, gr_s
        attn = _moba_sample(s_all, q, k, v, cache_vt, pt_flat, l, n_heads=n_heads, head_dim=head_dim)
        cs = state_conv[l].reshape(db, -1)
        rnn, h_last, cs_new = _rnn_sample(xr, gr, cs, state_rglru_h[l], *rnn_w)
        ys = _proj_out(ys, attn, ga, rnn, w_out_bf, tm=tms)
        outs["ks"].append(k.reshape(db, 1, n_heads, head_dim))
        outs["vs"].append(v.reshape(db, 1, n_heads, head_dim))
        outs["hs"].append(h_last)
        outs["cs"].append(cs_new.reshape(db, -1, dr))

    st = lambda n: jnp.stack(outs[n])
    return (yp.reshape(batch, seq, d), ys.reshape(db, 1, d), st("kp"), st("vp"), st("hp"),
            st("cp"), st("ks"), st("vs"), st("hs"), st("cs"))
```

```python
import functools

import jax
import jax.numpy as jnp
from jax import lax
from jax.experimental import pallas as pl
from jax.experimental.pallas import tpu as pltpu

ROPE_THETA = 500000.0
ROPE_FRACTION = 4
MOBA_BLOCK = 256
MOBA_TOPK = 3
LRU_C = 8.0
EPS = 1e-6
NEG = -1e30
LOG2E = 1.4426950408889634
PAGE_PHASE_CUTS = (0.0, 0.4375, 0.5, 0.9375, 1.0)
SELECT_SEQS_PER_STEP = 4

LANES = 128
SUBLANES = 8
VMEM_LIMIT_BYTES = 56 * 1024 * 1024

F32 = jnp.float32
BF16 = jnp.bfloat16
NT_DIMS = (((1,), (1,)), ((), ()))


def _cparams(sem):
    return pltpu.CompilerParams(dimension_semantics=sem, vmem_limit_bytes=VMEM_LIMIT_BYTES)


def _sigmoid(x):
    return 0.5 * jnp.tanh(0.5 * x) + 0.5


def _silu(x):
    return x * _sigmoid(x)


def _proj_in_kernel(x_ref, g_ref, w_ref, gq_ref, gk_ref, bd_ref, c_ref, s1_ref, s2_ref,
                    q_ref, k_ref, v_ref, ga_ref, xr_ref, gr_ref, *rest,
                    da, dr, head_dim, pos_tiles, with_prompt_outs):
    x = x_ref[...]
    ms = jnp.mean(x * x, axis=-1, keepdims=True)
    h = (x * lax.rsqrt(ms + EPS) * g_ref[...]).astype(BF16)

    cosf, sin_up, sin_dn = c_ref[...], s1_ref[...], s2_ref[...]
    bd = bd_ref[...]
    half = head_dim // ROPE_FRACTION // 2

    def proj(start, width):
        return jnp.dot(h, w_ref[:, start:start + width], preferred_element_type=F32)

    def head_norm_rope(t, gain):
        sq = t * t
        hi = sq.astype(BF16)
        lo = (sq - hi.astype(F32)).astype(BF16)
        cols = []
        for c in range(da // LANES):
            grp = slice(c * LANES, (c + 1) * LANES)
            ssum = (jnp.dot(hi[:, grp], bd, preferred_element_type=F32)
                    + jnp.dot(lo[:, grp], bd, preferred_element_type=F32))
            tc = t[:, grp] * lax.rsqrt(ssum * (1.0 / head_dim) + EPS) * gain[:, grp]
            up = pltpu.roll(tc, LANES - half, 1)
            dn = pltpu.roll(tc, half, 1)
            cols.append(tc * cosf + up * sin_up + dn * sin_dn)
        return jnp.concatenate(cols, axis=1)

    q_ref[...] = head_norm_rope(proj(0, da), gq_ref[...])
    k = head_norm_rope(proj(da, da), gk_ref[...])
    v = proj(2 * da, da)
    if with_prompt_outs:
        k_ref[0] = k.T
        v_ref[0] = v.T
    else:
        k_ref[...] = k
        v_ref[...] = v
    ga_ref[...] = proj(3 * da, da)
    xr_ref[...] = proj(4 * da, dr)
    gr_ref[...] = proj(4 * da + dr, dr)

    if with_prompt_outs:
        ka_ref, vb_ref, km_ref = rest
        tm = k.shape[0]
        vb_ref[...] = v.astype(BF16)
        lane = lax.broadcasted_iota(jnp.int32, (tm, LANES), 1)
        row_blk = lax.shift_right_logical(lax.broadcasted_iota(jnp.int32, (tm, LANES), 0),
                                          MOBA_BLOCK.bit_length() - 1)
        blk_id = (pl.program_id(0) % pos_tiles) * (tm // MOBA_BLOCK) + row_blk
        onehot = jnp.where(lane == blk_id, 1.0, 0.0).astype(BF16)
        kb = k.astype(BF16)
        ka_ref[...] = jnp.concatenate(
            [piece for c in range(da // LANES) for piece in (kb[:, c * LANES:(c + 1) * LANES], onehot)],
            axis=1)
        rows = [jnp.mean(k[b * MOBA_BLOCK:(b + 1) * MOBA_BLOCK], axis=0, keepdims=True)
                for b in range(tm // MOBA_BLOCK)]
        rows.append(jnp.zeros((SUBLANES - len(rows), da), F32))
        km_ref[0] = jnp.concatenate(rows, axis=0)


def _rope_tables(pos, head_dim):
    rope_dim = head_dim // ROPE_FRACTION
    half = rope_dim // 2
    inv = ROPE_THETA ** (-jnp.arange(half, dtype=F32) * 2.0 / rope_dim)
    ang = pos.astype(F32)[:, None] * inv
    cos, sin = jnp.cos(ang), jnp.sin(ang)
    n = pos.shape[0]
    ones = jnp.ones((n, head_dim - rope_dim), F32)
    zeros = jnp.zeros((n, head_dim - rope_dim), F32)
    zh = jnp.zeros((n, half), F32)
    reps = LANES // head_dim
    cosf = jnp.tile(jnp.concatenate([cos, cos, ones], axis=1), (1, reps))
    sin_up = jnp.tile(jnp.concatenate([-sin, zh, zeros], axis=1), (1, reps))
    sin_dn = jnp.tile(jnp.concatenate([zh, sin, zeros], axis=1), (1, reps))
    return cosf, sin_up, sin_dn


def _proj_in(x2d, pos, rms_g, gq, gk, w_bf, bd, *, da, dr, head_dim, tm, with_prompt_outs):
    rows, d = x2d.shape
    n_pos = pos.shape[0]
    assert rows % tm == 0 and n_pos % tm == 0
    pos_tiles = n_pos // tm
    cosf, sin_up, sin_dn = _rope_tables(pos, head_dim)
    reps = da // head_dim

    row_spec = lambda w: pl.BlockSpec((tm, w), lambda i: (i, 0))
    full = lambda a: pl.BlockSpec(a.shape, lambda i: (0,) * a.ndim)
    tab_spec = pl.BlockSpec((tm, LANES), lambda i: (i % pos_tiles, 0))

    g2 = rms_g.reshape(1, d)
    gq2 = jnp.tile(gq, reps).reshape(1, da)
    gk2 = jnp.tile(gk, reps).reshape(1, da)

    out_shape = [jax.ShapeDtypeStruct((rows, da), F32)] * 4 + [jax.ShapeDtypeStruct((rows, dr), F32)] * 2
    out_specs = [row_spec(da)] * 4 + [row_spec(dr)] * 2
    if with_prompt_outs:
        out_shape[1:3] = [jax.ShapeDtypeStruct((rows // n_pos, da, n_pos), F32)] * 2
        out_specs[1:3] = [pl.BlockSpec((1, da, tm), lambda i: (i // pos_tiles, 0, i % pos_tiles))] * 2
        assert tm % MOBA_BLOCK == 0 and tm // MOBA_BLOCK <= SUBLANES
        assert n_pos // MOBA_BLOCK <= LANES and MOBA_BLOCK & (MOBA_BLOCK - 1) == 0
        out_shape += [jax.ShapeDtypeStruct((rows, 2 * da), BF16)]
        out_shape += [jax.ShapeDtypeStruct((rows, da), BF16)]
        out_shape += [jax.ShapeDtypeStruct((rows // tm, SUBLANES, da), F32)]
        out_specs += [row_spec(2 * da), row_spec(da)]
        out_specs += [pl.BlockSpec((1, SUBLANES, da), lambda i: (i, 0, 0))]

    kern = functools.partial(_proj_in_kernel, da=da, dr=dr, head_dim=head_dim, pos_tiles=pos_tiles,
                             with_prompt_outs=with_prompt_outs)
    return pl.pallas_call(
        kern,
        grid=(rows // tm,),
        in_specs=[row_spec(d), full(g2), full(w_bf), full(gq2), full(gk2), full(bd),
                  tab_spec, tab_spec, tab_spec],
        out_specs=out_specs,
        out_shape=out_shape,
        compiler_params=_cparams(("arbitrary",)),
        name="proj_in",
    )(x2d, g2, w_bf, gq2, gk2, bd, cosf, sin_up, sin_dn)


def _proj_out_kernel(x_ref, *refs, da, half_tiles):
    if half_tiles:
        lo_ref, hi_ref, ga_ref, rnn_ref, w_ref, y_ref = refs
        in_low_half = pl.program_id(0) % (2 * half_tiles) < half_tiles
        attn = jnp.where(in_low_half, lo_ref[...], hi_ref[...])
    else:
        attn_ref, ga_ref, rnn_ref, w_ref, y_ref = refs
        attn = attn_ref[...]
    a = (attn * _silu(ga_ref[...])).astype(BF16)
    r = rnn_ref[...].astype(BF16)
    y = jnp.dot(a, w_ref[0:da, :], preferred_element_type=F32)
    y = y + jnp.dot(r, w_ref[da:, :], preferred_element_type=F32)
    y_ref[...] = x_ref[...] + y


def _proj_out(x2d, attn, ga, rnn, w_bf, *, tm, tiles_per_seq=None):
    rows, d = x2d.shape
    da, dr = ga.shape[1], rnn.shape[1]
    assert rows % tm == 0
    row_spec = lambda w: pl.BlockSpec((tm, w), lambda i: (i, 0))
    if isinstance(attn, tuple):
        assert tiles_per_seq % 2 == 0
        half_tiles = tiles_per_seq // 2
        seq_base = lambda i: (i // tiles_per_seq) * half_tiles
        attn_specs = [
            pl.BlockSpec((tm, da), lambda i: (seq_base(i) + jnp.minimum(i % tiles_per_seq,
                                                                        half_tiles - 1), 0)),
            pl.BlockSpec((tm, da), lambda i: (seq_base(i) + jnp.maximum(i % tiles_per_seq
                                                                        - half_tiles, 0), 0))]
        attn_args = attn
    else:
        half_tiles, attn_specs, attn_args = 0, [row_spec(da)], (attn,)
    return pl.pallas_call(
        functools.partial(_proj_out_kernel, da=da, half_tiles=half_tiles),
        grid=(rows // tm,),
        in_specs=[row_spec(d)] + attn_specs + [row_spec(da), row_spec(dr),
                                               pl.BlockSpec(w_bf.shape, lambda i: (0, 0))],
        out_specs=row_spec(d),
        out_shape=jax.ShapeDtypeStruct((rows, d), F32),
        compiler_params=_cparams(("arbitrary",)),
        name="proj_out",
    )(x2d, *attn_args, ga, rnn, w_bf)


def _lru_coeffs(xc, wax_ref, ba_ref, bx_ref, lam_ref, dr):
    pre = jnp.dot(xc.astype(BF16), wax_ref[...], preferred_element_type=F32)
    r = _sigmoid(pre[:, :dr] + ba_ref[...])
    gi = _sigmoid(pre[:, dr:] + bx_ref[...])
    z = -lam_ref[...]
    softplus = jnp.maximum(z, 0.0) + jnp.log1p(jnp.exp(-jnp.abs(z)))
    log_a = -LRU_C * r * softplus
    a = jnp.exp(log_a)
    gap = 1.0 - a * a
    mult = jnp.where(gap > 0.0, gap * lax.rsqrt(gap), 0.0)
    return a, mult * (gi * xc)


def _rnn_prompt_kernel(xr_ref, gr_ref, cw_ref, cb_ref, wax_ref, ba_ref, bx_ref, lam_ref,
                       out_ref, hl_ref, cbuf_ref, xe_ref, hc_ref, *, tr, dr, conv_w):
    t = pl.program_id(1)
    pad = SUBLANES

    @pl.when(t == 0)
    def _():
        xe_ref[0:pad, :] = jnp.zeros((pad, dr), F32)
        hc_ref[...] = jnp.zeros_like(hc_ref)

    x = xr_ref[...]
    xe_ref[pad:pad + tr, :] = x
    y = cb_ref[...] + cw_ref[conv_w - 1:conv_w, :] * x
    for j in range(conv_w - 1):
        back = conv_w - 1 - j
        y = y + cw_ref[j:j + 1, :] * xe_ref[pad - back:pad - back + tr, :]
    cbuf_ref[0] = xe_ref[pad + tr - (conv_w - 1):pad + tr, :]
    xe_ref[0:pad, :] = xe_ref[tr:tr + pad, :]

    a, b = _lru_coeffs(y, wax_ref, ba_ref, bx_ref, lam_ref, dr)

    grp = SUBLANES
    a = a.reshape(tr // grp, grp, dr)
    b = b.reshape(tr // grp, grp, dr)
    sub = lax.broadcasted_iota(jnp.int32, a.shape, 1)
    s = 1
    while s < grp:
        keep = sub >= s
        a_sh = pltpu.roll(a, s, 1)
        b_sh = pltpu.roll(b, s, 1)
        b = jnp.where(keep, a * b_sh + b, b)
        a = jnp.where(keep, a * a_sh, a)
        s *= 2
    h_prev = hc_ref[...]
    h_groups = []
    for g in range(tr // grp):
        h_g = a[g] * h_prev + b[g]
        h_groups.append(h_g)
        h_prev = h_g[grp - 1:grp, :]
    h = jnp.concatenate(h_groups, axis=0)
    h_last = h_prev
    hc_ref[...] = h_last
    hl_ref[0] = h_last
    out_ref[...] = h * _silu(gr_ref[...])


def _rnn_prompt(xr, gr, conv_w, conv_b, wax, b_a, b_x, lam, *, batch, seq, tr):
    rows, dr = xr.shape
    cw = conv_w.shape[0]
    assert seq % tr == 0 and rows == batch * seq and cw - 1 <= SUBLANES
    nt = seq // tr
    row_spec = pl.BlockSpec((tr, dr), lambda b, t: (b * nt + t, 0))
    full = lambda a: pl.BlockSpec(a.shape, lambda b, t: (0,) * a.ndim)
    vec = lambda a: a.reshape(1, dr)
    args = (xr, gr, conv_w, vec(conv_b), wax, vec(b_a), vec(b_x), vec(lam))
    return pl.pallas_call(
        functools.partial(_rnn_prompt_kernel, tr=tr, dr=dr, conv_w=cw),
        grid=(batch, nt),
        in_specs=[row_spec, row_spec] + [full(a) for a in args[2:]],
        out_specs=[row_spec,
                   pl.BlockSpec((1, 1, dr), lambda b, t: (b, 0, 0)),
                   pl.BlockSpec((1, cw - 1, dr), lambda b, t: (b, 0, 0))],
        out_shape=[jax.ShapeDtypeStruct((rows, dr), F32),
                   jax.ShapeDtypeStruct((batch, 1, dr), F32),
                   jax.ShapeDtypeStruct((batch, cw - 1, dr), F32)],
        scratch_shapes=[pltpu.VMEM((SUBLANES + tr, dr), F32), pltpu.VMEM((1, dr), F32)],
        compiler_params=_cparams(("arbitrary", "arbitrary")),
        name="rnn_prompt",
    )(*args)


def _rnn_sample_kernel(xr_ref, gr_ref, cs_ref, h0_ref, cw_ref, cb_ref, wax_ref, ba_ref, bx_ref,
                       lam_ref, out_ref, hl_ref, cs_out_ref, *, dr, conv_w):
    x = xr_ref[...]
    y = cb_ref[...] + cw_ref[conv_w - 1:conv_w, :] * x
    for j in range(conv_w - 1):
        y = y + cw_ref[j:j + 1, :] * cs_ref[:, j * dr:(j + 1) * dr]
    a, b = _lru_coeffs(y, wax_ref, ba_ref, bx_ref, lam_ref, dr)
    h = a * h0_ref[...] + b
    hl_ref[...] = h
    out_ref[...] = h * _silu(gr_ref[...])
    if conv_w > 2:
        cs_out_ref[:, 0:(conv_w - 2) * dr] = cs_ref[:, dr:(conv_w - 1) * dr]
    cs_out_ref[:, (conv_w - 2) * dr:] = x


def _rnn_sample(xr, gr, cs, h0, conv_w, conv_b, wax, b_a, b_x, lam):
    rows, dr = xr.shape
    cw = conv_w.shape[0]
    vec = lambda a: a.reshape(1, dr)
    args = (xr, gr, cs, h0, conv_w, vec(conv_b), wax, vec(b_a), vec(b_x), vec(lam))
    full = lambda a: pl.BlockSpec(a.shape, lambda i: (0,) * a.ndim)
    return pl.pallas_call(
        functools.partial(_rnn_sample_kernel, dr=dr, conv_w=cw),
        grid=(1,),
        in_specs=[full(a) for a in args],
        out_specs=[full(xr), full(xr), full(cs)],
        out_shape=[jax.ShapeDtypeStruct((rows, dr), F32), jax.ShapeDtypeStruct((rows, dr), F32),
                   jax.ShapeDtypeStruct(cs.shape, F32)],
        compiler_params=_cparams(("arbitrary",)),
        name="rnn_sample",
    )(*args)


def _first_max_pick(s, idx, axis, sentinel):
    m = jnp.max(s, axis=axis, keepdims=True)
    first = jnp.min(jnp.where((s == m) & (m > -jnp.inf), idx, sentinel), axis=axis, keepdims=True)
    return idx == first, first


def _bf16_parts(x):
    hi = x.astype(BF16)
    return hi, (x - hi.astype(F32)).astype(BF16)


def _moba_kernel(pt_ref, q_lo_ref, q_hi_ref, ka_ref, vb_ref, km_ref, qcol_ref, kt_hbm,
                 o_lo_ref, o_hi_ref, s_page_ref, s_sc, qa_sc, mx_sc, acc_sc, page_buf, page_sem, *,
                 head_dim, n_blocks, pages_per_step, layer):
    i = pl.program_id(2)
    g = (pl.program_id(0) * pl.num_programs(1) + pl.program_id(1)) * pl.num_programs(2) + i
    n_steps = pl.num_programs(0) * pl.num_programs(1) * pl.num_programs(2)
    slot = g % 2

    def page_copy(step, buf_slot, ii):
        return pltpu.make_async_copy(kt_hbm.at[layer, pt_ref[step * pages_per_step + ii]],
                                     page_buf.at[buf_slot, ii], page_sem.at[buf_slot])

    def start_pages(step, buf_slot):
        for ii in range(pages_per_step):
            page_copy(step, buf_slot, ii).start()

    def wait_pages(step, buf_slot):
        for ii in range(pages_per_step):
            page_copy(step, buf_slot, ii).wait()

    @pl.when(g == 0)
    def _():
        start_pages(0, 0)

    cuts = [round(pages_per_step * c) for c in PAGE_PHASE_CUTS]

    def score_pages(phase):
        _page_logits(page_buf, slot, cuts[phase], cuts[phase + 1], qcol_ref, s_page_ref,
                     head_dim ** -0.5)

    blk = MOBA_BLOCK
    qscale = head_dim ** -0.5 * LOG2E
    n_past = n_blocks - 1
    lane = lax.broadcasted_iota(jnp.int32, (blk, LANES), 1)
    n_hh = LANES // head_dim
    future = (lax.broadcasted_iota(jnp.int32, (blk, blk), 1)
              > lax.broadcasted_iota(jnp.int32, (blk, blk), 0))
    brow = lax.broadcasted_iota(jnp.int32, (n_blocks, blk), 0)
    km = km_ref[0]
    q_refs = (q_lo_ref, q_hi_ref)
    qblks = (i, n_blocks - 1 - i)

    def past_tile(j):
        second = j >= i
        start = pl.multiple_of(jnp.where(second, j - i, j) * blk, blk)
        return second.astype(jnp.int32), start

    km_parts = _bf16_parts(km)
    for hh in range(n_hh):
        in_head = (lane >= hh * head_dim) & (lane < (hh + 1) * head_dim)
        for w in range(2):
            qm = jnp.where(in_head, q_refs[w][...], 0.0)
            q_parts = _bf16_parts(qm)
            sb = sum(lax.dot_general(km_parts[a], q_parts[b], NT_DIMS, preferred_element_type=F32)
                     for a, b in ((0, 0), (0, 1), (1, 0)))
            sb = jnp.where(brow < qblks[w], sb, -jnp.inf)
            sel = brow == qblks[w]
            for _ in range(min(MOBA_TOPK, n_blocks)):
                pick, _ = _first_max_pick(sb, brow, 0, n_blocks)
                sel = sel | pick
                sb = jnp.where(pick, -jnp.inf, sb)
            bias_t = jnp.where(sel, 0.0, NEG)
            bias_t = jnp.concatenate([bias_t, jnp.full((LANES - n_blocks, blk), NEG, F32)], axis=0)
            qa_sc[w, hh * blk:(hh + 1) * blk, :] = jnp.concatenate(
                [(qm * qscale).astype(BF16), bias_t.T.astype(BF16)], axis=1)
    mx_sc[...] = jnp.full(mx_sc.shape, NEG, F32)
    acc_sc[...] = jnp.zeros(acc_sc.shape, F32)
    ones = jnp.ones((blk, LANES), BF16)

    def logits(which, start, slot, own):
        ka = ka_ref[pl.ds(start, blk), :]
        for hh in range(n_hh):
            rows = pl.ds(hh * blk, blk)
            s = lax.dot_general(qa_sc[which, rows, :], ka, NT_DIMS, preferred_element_type=F32)
            if own:
                s = jnp.where(future, NEG, s)
            s_sc[slot, rows, :] = s
            mx_sc[which, rows, :] = jnp.maximum(mx_sc[which, rows, :],
                                                jnp.maximum(s[:, :LANES], s[:, LANES:]))

    def weighted_values(which, start, slot):
        va = jnp.concatenate([vb_ref[pl.ds(start, blk), :], ones], axis=1)
        for hh in range(n_hh):
            rows = pl.ds(hh * blk, blk)
            m = mx_sc[which, rows, :]
            s = s_sc[slot, rows, :]
            p = jnp.concatenate([jnp.exp2(s[:, :LANES] - m), jnp.exp2(s[:, LANES:] - m)], axis=1)
            acc_sc[which, rows, :] = acc_sc[which, rows, :] + jnp.dot(
                p.astype(BF16), va, preferred_element_type=F32)

    own_tiles = [(w, pl.multiple_of(qblks[w] * blk, blk), n_past + w) for w in range(2)]

    def tile_pass(tile_fn, phase, first):
        @pl.when(i >= 0)
        def _():
            if first:
                wait_pages(g, slot)
            for j in range(n_past):
                which, start = past_tile(j)
                tile_fn(which, start, j, False)
            for which, start, tile_slot in own_tiles:
                tile_fn(which, start, tile_slot, True)
            score_pages(phase)

    tile_pass(logits, 0, True)
    for w in range(2):
        mx_sc[w] = jnp.broadcast_to(jnp.max(mx_sc[w], axis=1, keepdims=True), mx_sc.shape[1:])
    score_pages(1)
    start_pages(jnp.where(g + 1 < n_steps, g + 1, 0), 1 - slot)
    tile_pass(lambda which, start, tile_slot, own: weighted_values(which, start, tile_slot), 2, False)
    score_pages(3)

    @pl.when(g == n_steps - 1)
    def _():
        wait_pages(0, 1 - slot)

    for w, o_ref in enumerate((o_lo_ref, o_hi_ref)):
        out = None
        for hh in range(n_hh - 1, -1, -1):
            acc = acc_sc[w, hh * blk:(hh + 1) * blk, :]
            o_h = acc[:, :LANES] / acc[:, LANES:]
            out = o_h if out is None else jnp.where(lane < (hh + 1) * head_dim, o_h, out)
        o_ref[...] = out


def _moba_prompt(q, ka, vb, km, q_s, kt, pt_flat, layer, *, batch, seq, n_pages, head_dim):
    rows, da = q.shape
    assert LANES % head_dim == 0 and seq % MOBA_BLOCK == 0 and da % LANES == 0
    nb = seq // MOBA_BLOCK
    assert nb % SUBLANES == 0 and nb <= LANES
    blk = MOBA_BLOCK
    n_hh = LANES // head_dim
    half = nb // 2
    n_hp = da // LANES

    db = q_s.shape[0]
    n_heads, page = kt.shape[2], kt.shape[4]
    assert page == LANES
    n_steps = batch * n_hp * half
    total_pages = db * n_pages
    assert total_pages % n_steps == 0
    pps = total_pages // n_steps
    assert n_pages % pps == 0
    steps_per_seq = n_pages // pps
    qcol = jnp.broadcast_to(q_s.reshape(db, n_heads, head_dim, 1), (db, n_heads, head_dim, page))

    step = lambda b, hp, i: (b * n_hp + hp) * half + i
    q_spec = lambda f: pl.BlockSpec((blk, LANES), lambda b, hp, i, pt: (b * nb + f(i), hp))
    o_spec = lambda f: pl.BlockSpec((blk, LANES), lambda b, hp, i, pt: (b * half + f(i), hp))
    half_shape = jax.ShapeDtypeStruct((rows // 2, da), F32)
    lo, hi, s_all = pl.pallas_call(
        functools.partial(_moba_kernel, head_dim=head_dim, n_blocks=nb, pages_per_step=pps,
                          layer=layer),
        grid_spec=pltpu.PrefetchScalarGridSpec(
            num_scalar_prefetch=1,
            grid=(batch, n_hp, half),
            in_specs=[q_spec(lambda i: i), q_spec(lambda i: nb - 1 - i),
                      pl.BlockSpec((seq, 2 * LANES), lambda b, hp, i, pt: (b, hp)),
                      pl.BlockSpec((seq, LANES), lambda b, hp, i, pt: (b, hp)),
                      pl.BlockSpec((1, nb, LANES), lambda b, hp, i, pt: (b, 0, hp)),
                      pl.BlockSpec((1, n_heads, head_dim, page),
                                   lambda b, hp, i, pt: (step(b, hp, i) // steps_per_seq, 0, 0, 0)),
                      pl.BlockSpec(memory_space=pl.ANY)],
            out_specs=[o_spec(lambda i: i), o_spec(lambda i: half - 1 - i),
                       pl.BlockSpec((1, pps, n_heads, page),
                                    lambda b, hp, i, pt: (step(b, hp, i), 0, 0, 0))],
            scratch_shapes=[pltpu.VMEM((nb + 1, n_hh * blk, blk), F32),
                            pltpu.VMEM((2, n_hh * blk, 2 * LANES), BF16),
                            pltpu.VMEM((2, n_hh * blk, LANES), F32),
                            pltpu.VMEM((2, n_hh * blk, 2 * LANES), F32),
                            pltpu.VMEM((2, pps, n_heads, head_dim, page), F32),
                            pltpu.SemaphoreType.DMA((2,))]),
        out_shape=[half_shape, half_shape,
                   jax.ShapeDtypeStruct((n_steps, pps, n_heads, page), F32)],
        compiler_params=_cparams(("arbitrary", "arbitrary", "arbitrary")),
        name="moba_prompt",
    )(pt_flat, q, q, ka, vb, km, qcol, kt)
    return (lo, hi), s_all.reshape(db, n_pages, n_heads, page)


def _head_diag(n_heads, width, head_dim):
    sub = lax.broadcasted_iota(jnp.int32, (n_heads, width), 0)
    lane = lax.broadcasted_iota(jnp.int32, (n_heads, width), 1)
    return (lane >= sub * head_dim) & (lane < (sub + 1) * head_dim)


def _page_logits(page_buf, slot, first, last, qcol_ref, s_ref, scale):
    qcol = qcol_ref[0]
    for ii in range(first, last):
        s_ref[0, ii] = jnp.sum(page_buf[slot, ii] * qcol, axis=1) * scale


def _sample_select_kernel(s_ref, q_ref, kn_ref, p_ref, pown_ref, idx_ref, **static):
    for dd in range(s_ref.shape[0]):
        _sample_select_one(dd, s_ref, q_ref, kn_ref, p_ref, pown_ref, idx_ref, **static)


def _sample_select_one(dd, s_ref, q_ref, kn_ref, p_ref, pown_ref, idx_ref, *, ppb, n_heads, head_dim,
                       n_sel):
    da = n_heads * head_dim
    scale = head_dim ** -0.5
    s = s_ref[dd]
    n_blk = s.shape[0] // ppb
    tok = jnp.sum(s, axis=2, keepdims=True)
    sb = jnp.sum(tok.reshape(n_blk, ppb, n_heads, 1), axis=1)
    bidx = lax.broadcasted_iota(jnp.int32, sb.shape, 0)
    pblk = lax.shift_right_logical(
        lax.broadcasted_iota(jnp.int32, s.shape, 0), ppb.bit_length() - 1)
    lane = lax.broadcasted_iota(jnp.int32, (n_heads, LANES), 1)
    sel = jnp.zeros(s.shape, jnp.bool_)
    idx_out = jnp.zeros((n_heads, LANES), jnp.int32)
    for r in range(n_sel):
        pick, first = _first_max_pick(sb, bidx, 0, n_blk)
        sb = jnp.where(pick, -jnp.inf, sb)
        sel = sel | (pblk == first)
        idx_out = jnp.where(lane == r, first[0], idx_out)
    diag = _head_diag(n_heads, da, head_dim)
    qk_new = jnp.broadcast_to(q_ref[dd] * kn_ref[dd], (n_heads, da))
    s_own = jnp.sum(jnp.where(diag, qk_new, 0.0), axis=1, keepdims=True) * scale
    sm = jnp.where(sel, s, -jnp.inf)
    m = jnp.max(jnp.max(sm, axis=0), axis=1, keepdims=True)
    m = jnp.maximum(m, s_own)
    p = jnp.where(sel, jnp.exp(s - m[None]), 0.0)
    p_own = jnp.exp(s_own - m)
    den = jnp.sum(jnp.sum(p, axis=0), axis=1, keepdims=True) + p_own
    inv = 1.0 / den
    p_ref[dd] = p * inv[None]
    pown_ref[dd] = jnp.broadcast_to(p_own * inv, (n_heads, LANES))
    idx_ref[dd] = idx_out


def _sample_value_kernel(pt_ref, sel_ref, p_ref, pown_ref, vn_ref, *rest, n_sel, ppb, n_heads):
    v_refs = rest[:-1]
    o_ref = rest[-1]
    d = pl.program_id(0)
    page = p_ref.shape[-1]
    head_dim = o_ref.shape[-1]
    lane = lax.broadcasted_iota(jnp.int32, (head_dim, page), 1)
    cols = jnp.zeros((head_dim, page), F32)
    for h in range(n_heads):
        acc = None
        for r in range(n_sel):
            s = h * n_sel + r
            blk = sel_ref[d * (n_heads * n_sel) + s]
            for u in range(ppb):
                pp = p_ref[0, blk * ppb + u, h:h + 1, :]
                term = v_refs[s * ppb + u][0, 0, 0] * pp
                acc = term if acc is None else acc + term
        cols = jnp.where(lane == h, jnp.sum(acc, axis=1, keepdims=True), cols)
    if head_dim < page:
        cols = jnp.concatenate([cols, jnp.zeros((page - head_dim, page), F32)], axis=0)
    out = cols.T[:n_heads, :head_dim]
    o_ref[0] = out + pown_ref[0][:, 0:1] * vn_ref[0]


def _moba_sample(s_all, q, k_new, v_new, vt, pt_flat, layer, *, n_heads, head_dim):
    db, da = q.shape
    n_pages, page = s_all.shape[1], s_all.shape[3]
    ppb = MOBA_BLOCK // page
    assert ppb >= 1 and ppb & (ppb - 1) == 0 and MOBA_BLOCK % page == 0
    assert (n_pages * page) % MOBA_BLOCK == 0
    n_blk = n_pages // ppb
    n_sel = min(MOBA_TOPK, n_blk)
    assert n_sel > 0 and n_sel <= LANES
    q3, kn3 = (a.reshape(db, 1, da) for a in (q, k_new))
    vn3 = v_new.reshape(db, n_heads, head_dim)

    nd = _pick_tile(db, SELECT_SEQS_PER_STEP)
    row3 = pl.BlockSpec((nd, 1, da), lambda d: (d, 0, 0))
    head_spec = pl.BlockSpec((nd, n_heads, LANES), lambda d: (d, 0, 0))
    s_spec = pl.BlockSpec((nd, n_pages, n_heads, page), lambda d: (d, 0, 0, 0))
    p_all, p_own, idx = pl.pallas_call(
        functools.partial(_sample_select_kernel, ppb=ppb, n_heads=n_heads, head_dim=head_dim,
                          n_sel=n_sel),
        grid=(db // nd,),
        in_specs=[s_spec, row3, row3],
        out_specs=[s_spec, head_spec, head_spec],
        out_shape=[jax.ShapeDtypeStruct(s_all.shape, F32),
                   jax.ShapeDtypeStruct((db, n_heads, LANES), F32),
                   jax.ShapeDtypeStruct((db, n_heads, LANES), jnp.int32)],
        compiler_params=_cparams(("arbitrary",)),
        name="sample_select",
    )(s_all, q3, kn3)

    sel_flat = idx[:, :, :n_sel].reshape(-1)
    n_fetch = n_heads * n_sel * ppb
    head3 = pl.BlockSpec((1, n_heads, head_dim), lambda d, pt, sel: (d, 0, 0))

    def v_slab(d, pt, sel, s, u):
        d = jnp.minimum(d, db - 1)
        blk = jnp.clip(sel[d * (n_heads * n_sel) + s], 0, n_blk - 1)
        return (layer, pt[d * n_pages + blk * ppb + u], s // n_sel, 0, 0)

    v_specs = [pl.BlockSpec((1, 1, 1, head_dim, page),
                            functools.partial(v_slab, s=f // ppb, u=f % ppb))
               for f in range(n_fetch)]
    out = pl.pallas_call(
        functools.partial(_sample_value_kernel, n_sel=n_sel, ppb=ppb, n_heads=n_heads),
        grid_spec=pltpu.PrefetchScalarGridSpec(
            num_scalar_prefetch=2,
            grid=(db,),
            in_specs=[pl.BlockSpec((1, n_pages, n_heads, page), lambda d, pt, sel: (d, 0, 0, 0)),
                      pl.BlockSpec((1, n_heads, LANES), lambda d, pt, sel: (d, 0, 0)),
                      head3] + v_specs,
            out_specs=head3),
        out_shape=jax.ShapeDtypeStruct((db, n_heads, head_dim), F32),
        compiler_params=_cparams(("arbitrary",)),
        name="sample_value",
    )(pt_flat, sel_flat, p_all, p_own, vn3, *([vt] * n_fetch))
    return out.reshape(db, da)


def _block_diag(w):
    n, wi, wo = w.shape
    eye = jnp.eye(n, dtype=w.dtype)
    return (eye[:, None, :, None] * w[:, :, None, :]).reshape(n * wi, n * wo)


def _pick_tile(n, cap):
    t = min(n, cap)
    while n % t:
        t //= 2
    return t


def kernel(x_prompt, x_sample, cache_k, cache_v, state_rglru_h, state_conv, page_table, rms_g,
           q_norm_g, k_norm_g, w_in, conv_w, conv_b, w_a, b_a, w_x, b_x, lru_lambda, w_out):
    depth = w_in.shape[0]
    batch, seq, d = x_prompt.shape
    db, dec_seq, _ = x_sample.shape
    _, n_pool, page, n_heads, head_dim = cache_k.shape
    da = n_heads * head_dim
    dr = state_rglru_h.shape[-1]
    n_pages = page_table.shape[1]
    past_len = n_pages * page
    assert dec_seq == 1, "sample group is one new token per sequence"
    assert w_in.shape[2] == 4 * da + 2 * dr and da % LANES == 0 and dr % LANES == 0

    tm = _pick_tile(seq, 512)
    tms = _pick_tile(db, 512)
    pos_p = jnp.arange(seq)
    pos_s = jnp.full((db,), past_len, jnp.int32)
    bd = _block_diag(jnp.ones((LANES // head_dim, head_dim, head_dim), BF16))
    cache_kt = jnp.transpose(cache_k, (0, 1, 3, 4, 2))
    cache_vt = jnp.transpose(cache_v, (0, 1, 3, 4, 2))
    pt_flat = page_table.reshape(-1).astype(jnp.int32)

    yp = x_prompt.reshape(batch * seq, d)
    ys = x_sample.reshape(db, d)
    outs = {n: [] for n in ("kp", "vp", "hp", "cp", "ks", "vs", "hs", "cs")}
    for l in range(depth):
        w_in_bf = w_in[l].astype(BF16)
        w_out_bf = w_out[l].astype(BF16)
        wax = jnp.concatenate([_block_diag(w_a[l]), _block_diag(w_x[l])], axis=1).astype(BF16)
        proj = functools.partial(_proj_in, rms_g=rms_g[l], gq=q_norm_g[l], gk=k_norm_g[l],
                                 w_bf=w_in_bf, bd=bd, da=da, dr=dr, head_dim=head_dim)
        rnn_w = (conv_w[l], conv_b[l], wax, b_a[l], b_x[l], lru_lambda[l])

        q, k, v, ga, xr, gr, ka, vb, km = proj(yp, pos_p, tm=tm, with_prompt_outs=True)
        q_s, k_s, v_s, ga_s, xr_s, gr_s = proj(ys, pos_s, tm=tms, with_prompt_outs=False)

        km = km[:, :tm // MOBA_BLOCK, :].reshape(batch, seq // MOBA_BLOCK, da)
        attn, s_all = _moba_prompt(q, ka, vb, km, q_s, cache_kt, pt_flat, l, batch=batch, seq=seq,
                                   n_pages=n_pages, head_dim=head_dim)
        rnn, h_last, cbuf = _rnn_prompt(xr, gr, *rnn_w, batch=batch, seq=seq, tr=tm)
        yp = _proj_out(yp, attn, ga, rnn, w_out_bf, tm=tm, tiles_per_seq=seq // tm)
        to_out = lambda t: t.reshape(batch, n_heads, head_dim, seq).transpose(0, 3, 1, 2)
        outs["kp"].append(to_out(k))
        outs["vp"].append(to_out(v))
        outs["hp"].append(h_last.reshape(batch, dr))
        outs["cp"].append(cbuf)

        q, k, v, ga, xr, gr = q_s, k_s, v_s, ga_s, xr_s, gr_s
        attn = _moba_sample(s_all, q, k, v, cache_vt, pt_flat, l, n_heads=n_heads, head_dim=head_dim)
        cs = state_conv[l].reshape(db, -1)
        rnn, h_last, cs_new = _rnn_sample(xr, gr, cs, state_rglru_h[l], *rnn_w)
        ys = _proj_out(ys, attn, ga, rnn, w_out_bf, tm=tms)
        outs["ks"].append(k.reshape(db, 1, n_heads, head_dim))
        outs["vs"].append(v.reshape(db, 1, n_heads, head_dim))
        outs["hs"].append(h_last)
        outs["cs"].append(cs_new.reshape(db, -1, dr))

    st = lambda n: jnp.stack(outs[n])
    return (yp.reshape(batch, seq, d), ys.reshape(db, 1, d), st("kp"), st("vp"), st("hp"),
            st("cp"), st("ks"), st("vs"), st("hs"), st("cs"))
```

```python
import functools

import jax
import jax.numpy as jnp
from jax import lax
from jax.experimental import pallas as pl
from jax.experimental.pallas import tpu as pltpu

ROPE_THETA = 500000.0
ROPE_FRACTION = 4
MOBA_BLOCK = 256
MOBA_TOPK = 3
LRU_C = 8.0
EPS = 1e-6
NEG = -1e30
LOG2E = 1.4426950408889634
PAGE_PHASE_CUTS = (0.0, 0.4375, 0.5, 0.9375, 1.0)
SELECT_SEQS_PER_STEP = 4

LANES = 128
SUBLANES = 8
VMEM_LIMIT_BYTES = 56 * 1024 * 1024

F32 = jnp.float32
BF16 = jnp.bfloat16
NT_DIMS = (((1,), (1,)), ((), ()))


def _cparams(sem):
    return pltpu.CompilerParams(dimension_semantics=sem, vmem_limit_bytes=VMEM_LIMIT_BYTES)


def _sigmoid(x):
    return 0.5 * jnp.tanh(0.5 * x) + 0.5


def _silu(x):
    return x * _sigmoid(x)


def _proj_in_kernel(x_ref, g_ref, w_ref, gq_ref, gk_ref, bd_ref, c_ref, s1_ref, s2_ref,
                    q_ref, k_ref, v_ref, ga_ref, xr_ref, gr_ref, *rest,
                    da, dr, head_dim, pos_tiles, with_prompt_outs):
    x = x_ref[...]
    ms = jnp.mean(x * x, axis=-1, keepdims=True)
    h = (x * lax.rsqrt(ms + EPS) * g_ref[...]).astype(BF16)

    cosf, sin_up, sin_dn = c_ref[...], s1_ref[...], s2_ref[...]
    bd = bd_ref[...]
    half = head_dim // ROPE_FRACTION // 2

    def proj(start, width):
        return jnp.dot(h, w_ref[:, start:start + width], preferred_element_type=F32)

    def head_norm_rope(t, gain):
        sq = t * t
        hi = sq.astype(BF16)
        lo = (sq - hi.astype(F32)).astype(BF16)
        cols = []
        for c in range(da // LANES):
            grp = slice(c * LANES, (c + 1) * LANES)
            ssum = (jnp.dot(hi[:, grp], bd, preferred_element_type=F32)
                    + jnp.dot(lo[:, grp], bd, preferred_element_type=F32))
            tc = t[:, grp] * lax.rsqrt(ssum * (1.0 / head_dim) + EPS) * gain[:, grp]
            up = pltpu.roll(tc, LANES - half, 1)
            dn = pltpu.roll(tc, half, 1)
            cols.append(tc * cosf + up * sin_up + dn * sin_dn)
        return jnp.concatenate(cols, axis=1)

    q_ref[...] = head_norm_rope(proj(0, da), gq_ref[...])
    k = head_norm_rope(proj(da, da), gk_ref[...])
    v = proj(2 * da, da)
    if with_prompt_outs:
        k_ref[0] = k.T
        v_ref[0] = v.T
    else:
        k_ref[...] = k
        v_ref[...] = v
    ga_ref[...] = proj(3 * da, da)
    xr_ref[...] = proj(4 * da, dr)
    gr_ref[...] = proj(4 * da + dr, dr)

    if with_prompt_outs:
        ka_ref, vb_ref, km_ref = rest
        tm = k.shape[0]
        vb_ref[...] = v.astype(BF16)
        lane = lax.broadcasted_iota(jnp.int32, (tm, LANES), 1)
        row_blk = lax.shift_right_logical(lax.broadcasted_iota(jnp.int32, (tm, LANES), 0),
                                          MOBA_BLOCK.bit_length() - 1)
        blk_id = (pl.program_id(0) % pos_tiles) * (tm // MOBA_BLOCK) + row_blk
        onehot = jnp.where(lane == blk_id, 1.0, 0.0).astype(BF16)
        kb = k.astype(BF16)
        ka_ref[...] = jnp.concatenate(
            [piece for c in range(da // LANES) for piece in (kb[:, c * LANES:(c + 1) * LANES], onehot)],
            axis=1)
        rows = [jnp.mean(k[b * MOBA_BLOCK:(b + 1) * MOBA_BLOCK], axis=0, keepdims=True)
                for b in range(tm // MOBA_BLOCK)]
        rows.append(jnp.zeros((SUBLANES - len(rows), da), F32))
        km_ref[0] = jnp.concatenate(rows, axis=0)


def _rope_tables(pos, head_dim):
    rope_dim = head_dim // ROPE_FRACTION
    half = rope_dim // 2
    inv = ROPE_THETA ** (-jnp.arange(half, dtype=F32) * 2.0 / rope_dim)
    ang = pos.astype(F32)[:, None] * inv
    cos, sin = jnp.cos(ang), jnp.sin(ang)
    n = pos.shape[0]
    ones = jnp.ones((n, head_dim - rope_dim), F32)
    zeros = jnp.zeros((n, head_dim - rope_dim), F32)
    zh = jnp.zeros((n, half), F32)
    reps = LANES // head_dim
    cosf = jnp.tile(jnp.concatenate([cos, cos, ones], axis=1), (1, reps))
    sin_up = jnp.tile(jnp.concatenate([-sin, zh, zeros], axis=1), (1, reps))
    sin_dn = jnp.tile(jnp.concatenate([zh, sin, zeros], axis=1), (1, reps))
    return cosf, sin_up, sin_dn


def _proj_in(x2d, pos, rms_g, gq, gk, w_bf, bd, *, da, dr, head_dim, tm, with_prompt_outs):
    rows, d = x2d.shape
    n_pos = pos.shape[0]
    assert rows % tm == 0 and n_pos % tm == 0
    pos_tiles = n_pos // tm
    cosf, sin_up, sin_dn = _rope_tables(pos, head_dim)
    reps = da // head_dim

    row_spec = lambda w: pl.BlockSpec((tm, w), lambda i: (i, 0))
    full = lambda a: pl.BlockSpec(a.shape, lambda i: (0,) * a.ndim)
    tab_spec = pl.BlockSpec((tm, LANES), lambda i: (i % pos_tiles, 0))

    g2 = rms_g.reshape(1, d)
    gq2 = jnp.tile(gq, reps).reshape(1, da)
    gk2 = jnp.tile(gk, reps).reshape(1, da)

    out_shape = [jax.ShapeDtypeStruct((rows, da), F32)] * 4 + [jax.ShapeDtypeStruct((rows, dr), F32)] * 2
    out_specs = [row_spec(da)] * 4 + [row_spec(dr)] * 2
    if with_prompt_outs:
        out_shape[1:3] = [jax.ShapeDtypeStruct((rows // n_pos, da, n_pos), F32)] * 2
        out_specs[1:3] = [pl.BlockSpec((1, da, tm), lambda i: (i // pos_tiles, 0, i % pos_tiles))] * 2
        assert tm % MOBA_BLOCK == 0 and tm // MOBA_BLOCK <= SUBLANES
        assert n_pos // MOBA_BLOCK <= LANES and MOBA_BLOCK & (MOBA_BLOCK - 1) == 0
        out_shape += [jax.ShapeDtypeStruct((rows, 2 * da), BF16)]
        out_shape += [jax.ShapeDtypeStruct((rows, da), BF16)]
        out_shape += [jax.ShapeDtypeStruct((rows // tm, SUBLANES, da), F32)]
        out_specs += [row_spec(2 * da), row_spec(da)]
        out_specs += [pl.BlockSpec((1, SUBLANES, da), lambda i: (i, 0, 0))]

    kern = functools.partial(_proj_in_kernel, da=da, dr=dr, head_dim=head_dim, pos_tiles=pos_tiles,
                             with_prompt_outs=with_prompt_outs)
    return pl.pallas_call(
        kern,
        grid=(rows // tm,),
        in_specs=[row_spec(d), full(g2), full(w_bf), full(gq2), full(gk2), full(bd),
                  tab_spec, tab_spec, tab_spec],
        out_specs=out_specs,
        out_shape=out_shape,
        compiler_params=_cparams(("arbitrary",)),
        name="proj_in",
    )(x2d, g2, w_bf, gq2, gk2, bd, cosf, sin_up, sin_dn)


def _proj_out_kernel(x_ref, *refs, da, half_tiles):
    if half_tiles:
        lo_ref, hi_ref, ga_ref, rnn_ref, w_ref, y_ref = refs
        in_low_half = pl.program_id(0) % (2 * half_tiles) < half_tiles
        attn = jnp.where(in_low_half, lo_ref[...], hi_ref[...])
    else:
        attn_ref, ga_ref, rnn_ref, w_ref, y_ref = refs
        attn = attn_ref[...]
    a = (attn * _silu(ga_ref[...])).astype(BF16)
    r = rnn_ref[...].astype(BF16)
    y = jnp.dot(a, w_ref[0:da, :], preferred_element_type=F32)
    y = y + jnp.dot(r, w_ref[da:, :], preferred_element_type=F32)
    y_ref[...] = x_ref[...] + y


def _proj_out(x2d, attn, ga, rnn, w_bf, *, tm, tiles_per_seq=None):
    rows, d = x2d.shape
    da, dr = ga.shape[1], rnn.shape[1]
    assert rows % tm == 0
    row_spec = lambda w: pl.BlockSpec((tm, w), lambda i: (i, 0))
    if isinstance(attn, tuple):
        assert tiles_per_seq % 2 == 0
        half_tiles = tiles_per_seq // 2
        seq_base = lambda i: (i // tiles_per_seq) * half_tiles
        attn_specs = [
            pl.BlockSpec((tm, da), lambda i: (seq_base(i) + jnp.minimum(i % tiles_per_seq,
                                                                        half_tiles - 1), 0)),
            pl.BlockSpec((tm, da), lambda i: (seq_base(i) + jnp.maximum(i % tiles_per_seq
                                                                        - half_tiles, 0), 0))]
        attn_args = attn
    else:
        half_tiles, attn_specs, attn_args = 0, [row_spec(da)], (attn,)
    return pl.pallas_call(
        functools.partial(_proj_out_kernel, da=da, half_tiles=half_tiles),
        grid=(rows // tm,),
        in_specs=[row_spec(d)] + attn_specs + [row_spec(da), row_spec(dr),
                                               pl.BlockSpec(w_bf.shape, lambda i: (0, 0))],
        out_specs=row_spec(d),
        out_shape=jax.ShapeDtypeStruct((rows, d), F32),
        compiler_params=_cparams(("arbitrary",)),
        name="proj_out",
    )(x2d, *attn_args, ga, rnn, w_bf)


def _lru_coeffs(xc, wax_ref, ba_ref, bx_ref, lam_ref, dr):
    pre = jnp.dot(xc.astype(BF16), wax_ref[...], preferred_element_type=F32)
    r = _sigmoid(pre[:, :dr] + ba_ref[...])
    gi = _sigmoid(pre[:, dr:] + bx_ref[...])
    z = -lam_ref[...]
    softplus = jnp.maximum(z, 0.0) + jnp.log1p(jnp.exp(-jnp.abs(z)))
    log_a = -LRU_C * r * softplus
    a = jnp.exp(log_a)
    gap = 1.0 - a * a
    mult = jnp.where(gap > 0.0, gap * lax.rsqrt(gap), 0.0)
    return a, mult * (gi * xc)


def _rnn_prompt_kernel(xr_ref, gr_ref, cw_ref, cb_ref, wax_ref, ba_ref, bx_ref, lam_ref,
                       out_ref, hl_ref, cbuf_ref, xe_ref, hc_ref, *, tr, dr, conv_w):
    t = pl.program_id(1)
    pad = SUBLANES

    @pl.when(t == 0)
    def _():
        xe_ref[0:pad, :] = jnp.zeros((pad, dr), F32)
        hc_ref[...] = jnp.zeros_like(hc_ref)

    x = xr_ref[...]
    xe_ref[pad:pad + tr, :] = x
    y = cb_ref[...] + cw_ref[conv_w - 1:conv_w, :] * x
    for j in range(conv_w - 1):
        back = conv_w - 1 - j
        y = y + cw_ref[j:j + 1, :] * xe_ref[pad - back:pad - back + tr, :]
    cbuf_ref[0] = xe_ref[pad + tr - (conv_w - 1):pad + tr, :]
    xe_ref[0:pad, :] = xe_ref[tr:tr + pad, :]

    a, b = _lru_coeffs(y, wax_ref, ba_ref, bx_ref, lam_ref, dr)

    grp = SUBLANES
    a = a.reshape(tr // grp, grp, dr)
    b = b.reshape(tr // grp, grp, dr)
    sub = lax.broadcasted_iota(jnp.int32, a.shape, 1)
    s = 1
    while s < grp:
        keep = sub >= s
        a_sh = pltpu.roll(a, s, 1)
        b_sh = pltpu.roll(b, s, 1)
        b = jnp.where(keep, a * b_sh + b, b)
        a = jnp.where(keep, a * a_sh, a)
        s *= 2
    h_prev = hc_ref[...]
    h_groups = []
    for g in range(tr // grp):
        h_g = a[g] * h_prev + b[g]
        h_groups.append(h_g)
        h_prev = h_g[grp - 1:grp, :]
    h = jnp.concatenate(h_groups, axis=0)
    h_last = h_prev
    hc_ref[...] = h_last
    hl_ref[0] = h_last
    out_ref[...] = h * _silu(gr_ref[...])


def _rnn_prompt(xr, gr, conv_w, conv_b, wax, b_a, b_x, lam, *, batch, seq, tr):
    rows, dr = xr.shape
    cw = conv_w.shape[0]
    assert seq % tr == 0 and rows == batch * seq and cw - 1 <= SUBLANES
    nt = seq // tr
    row_spec = pl.BlockSpec((tr, dr), lambda b, t: (b * nt + t, 0))
    full = lambda a: pl.BlockSpec(a.shape, lambda b, t: (0,) * a.ndim)
    vec = lambda a: a.reshape(1, dr)
    args = (xr, gr, conv_w, vec(conv_b), wax, vec(b_a), vec(b_x), vec(lam))
    return pl.pallas_call(
        functools.partial(_rnn_prompt_kernel, tr=tr, dr=dr, conv_w=cw),
        grid=(batch, nt),
        in_specs=[row_spec, row_spec] + [full(a) for a in args[2:]],
        out_specs=[row_spec,
                   pl.BlockSpec((1, 1, dr), lambda b, t: (b, 0, 0)),
                   pl.BlockSpec((1, cw - 1, dr), lambda b, t: (b, 0, 0))],
        out_shape=[jax.ShapeDtypeStruct((rows, dr), F32),
                   jax.ShapeDtypeStruct((batch, 1, dr), F32),
                   jax.ShapeDtypeStruct((batch, cw - 1, dr), F32)],
        scratch_shapes=[pltpu.VMEM((SUBLANES + tr, dr), F32), pltpu.VMEM((1, dr), F32)],
        compiler_params=_cparams(("arbitrary", "arbitrary")),
        name="rnn_prompt",
    )(*args)


def _rnn_sample_kernel(xr_ref, gr_ref, cs_ref, h0_ref, cw_ref, cb_ref, wax_ref, ba_ref, bx_ref,
                       lam_ref, out_ref, hl_ref, cs_out_ref, *, dr, conv_w):
    x = xr_ref[...]
    y = cb_ref[...] + cw_ref[conv_w - 1:conv_w, :] * x
    for j in range(conv_w - 1):
        y = y + cw_ref[j:j + 1, :] * cs_ref[:, j * dr:(j + 1) * dr]
    a, b = _lru_coeffs(y, wax_ref, ba_ref, bx_ref, lam_ref, dr)
    h = a * h0_ref[...] + b
    hl_ref[...] = h
    out_ref[...] = h * _silu(gr_ref[...])
    if conv_w > 2:
        cs_out_ref[:, 0:(conv_w - 2) * dr] = cs_ref[:, dr:(conv_w - 1) * dr]
    cs_out_ref[:, (conv_w - 2) * dr:] = x


def _rnn_sample(xr, gr, cs, h0, conv_w, conv_b, wax, b_a, b_x, lam):
    rows, dr = xr.shape
    cw = conv_w.shape[0]
    vec = lambda a: a.reshape(1, dr)
    args = (xr, gr, cs, h0, conv_w, vec(conv_b), wax, vec(b_a), vec(b_x), vec(lam))
    full = lambda a: pl.BlockSpec(a.shape, lambda i: (0,) * a.ndim)
    return pl.pallas_call(
        functools.partial(_rnn_sample_kernel, dr=dr, conv_w=cw),
        grid=(1,),
        in_specs=[full(a) for a in args],
        out_specs=[full(xr), full(xr), full(cs)],
        out_shape=[jax.ShapeDtypeStruct((rows, dr), F32), jax.ShapeDtypeStruct((rows, dr), F32),
                   jax.ShapeDtypeStruct(cs.shape, F32)],
        compiler_params=_cparams(("arbitrary",)),
        name="rnn_sample",
    )(*args)


def _first_max_pick(s, idx, axis, sentinel):
    m = jnp.max(s, axis=axis, keepdims=True)
    first = jnp.min(jnp.where((s == m) & (m > -jnp.inf), idx, sentinel), axis=axis, keepdims=True)
    return idx == first, first


def _bf16_parts(x):
    hi = x.astype(BF16)
    return hi, (x - hi.astype(F32)).astype(BF16)


def _moba_kernel(pt_ref, q_lo_ref, q_hi_ref, ka_ref, vb_ref, km_ref, qcol_ref, kt_hbm,
                 o_lo_ref, o_hi_ref, s_page_ref, s_sc, qa_sc, mx_sc, acc_sc, page_buf, page_sem, *,
                 head_dim, n_blocks, pages_per_step, layer):
    i = pl.program_id(2)
    g = (pl.program_id(0) * pl.num_programs(1) + pl.program_id(1)) * pl.num_programs(2) + i
    n_steps = pl.num_programs(0) * pl.num_programs(1) * pl.num_programs(2)
    slot = g % 2

    def page_copy(step, buf_slot, ii):
        return pltpu.make_async_copy(kt_hbm.at[layer, pt_ref[step * pages_per_step + ii]],
                                     page_buf.at[buf_slot, ii], page_sem.at[buf_slot])

    def start_pages(step, buf_slot):
        for ii in range(pages_per_step):
            page_copy(step, buf_slot, ii).start()

    def wait_pages(step, buf_slot):
        for ii in range(pages_per_step):
            page_copy(step, buf_slot, ii).wait()

    @pl.when(g == 0)
    def _():
        start_pages(0, 0)

    start_pages(jnp.where(g + 1 < n_steps, g + 1, 0), 1 - slot)

    cuts =[round(pages_per_step * c) for c in PAGE_PHASE_CUTS]

    def score_pages(phase):
        _page_logits(page_buf, slot, cuts[phase], cuts[phase + 1], qcol_ref, s_page_ref,
                     head_dim ** -0.5)

    blk = MOBA_BLOCK
    qscale = head_dim ** -0.5 * LOG2E
    n_past = n_blocks - 1
    lane = lax.broadcasted_iota(jnp.int32, (blk, LANES), 1)
    n_hh = LANES // head_dim
    future = (lax.broadcasted_iota(jnp.int32, (blk, blk), 1)
              > lax.broadcasted_iota(jnp.int32, (blk, blk), 0))
    brow = lax.broadcasted_iota(jnp.int32, (n_blocks, blk), 0)
    km = km_ref[0]
    q_refs = (q_lo_ref, q_hi_ref)
    qblks = (i, n_blocks - 1 - i)

    def past_tile(j):
        second = j >= i
        start = pl.multiple_of(jnp.where(second, j - i, j) * blk, blk)
        return second.astype(jnp.int32), start

    km_parts = _bf16_parts(km)
    for hh in range(n_hh):
        in_head = (lane >= hh * head_dim) & (lane < (hh + 1) * head_dim)
        for w in range(2):
            qm = jnp.where(in_head, q_refs[w][...], 0.0)
            q_parts = _bf16_parts(qm)
            sb = sum(lax.dot_general(km_parts[a], q_parts[b], NT_DIMS, preferred_element_type=F32)
                     for a, b in ((0, 0), (0, 1), (1, 0)))
            sb = jnp.where(brow < qblks[w], sb, -jnp.inf)
            sel = brow == qblks[w]
            for _ in range(min(MOBA_TOPK, n_blocks)):
                pick, _ = _first_max_pick(sb, brow, 0, n_blocks)
                sel = sel | pick
                sb = jnp.where(pick, -jnp.inf, sb)
            bias_t = jnp.where(sel, 0.0, NEG)
            bias_t = jnp.concatenate([bias_t, jnp.full((LANES - n_blocks, blk), NEG, F32)], axis=0)
            qa_sc[w, hh * blk:(hh + 1) * blk, :] = jnp.concatenate(
                [(qm * qscale).astype(BF16), bias_t.T.astype(BF16)], axis=1)
    mx_sc[...] = jnp.full(mx_sc.shape, NEG, F32)
    acc_sc[...] = jnp.zeros(acc_sc.shape, F32)
    ones = jnp.ones((blk, LANES), BF16)

    def logits(which, start, slot, own):
        ka = ka_ref[pl.ds(start, blk), :]
        for hh in range(n_hh):
            rows = pl.ds(hh * blk, blk)
            s = lax.dot_general(qa_sc[which, rows, :], ka, NT_DIMS, preferred_element_type=F32)
            if own:
                s = jnp.where(future, NEG, s)
            s_sc[slot, rows, :] = s
            mx_sc[which, rows, :] = jnp.maximum(mx_sc[which, rows, :],
                                                jnp.maximum(s[:, :LANES], s[:, LANES:]))

    def weighted_values(which, start, slot):
        va = jnp.concatenate([vb_ref[pl.ds(start, blk), :], ones], axis=1)
        for hh in range(n_hh):
            rows = pl.ds(hh * blk, blk)
            m = mx_sc[which, rows, :]
            s = s_sc[slot, rows, :]
            p = jnp.concatenate([jnp.exp2(s[:, :LANES] - m), jnp.exp2(s[:, LANES:] - m)], axis=1)
            acc_sc[which, rows, :] = acc_sc[which, rows, :] + jnp.dot(
                p.astype(BF16), va, preferred_element_type=F32)

    own_tiles = [(w, pl.multiple_of(qblks[w] * blk, blk), n_past + w) for w in range(2)]

    def tile_pass(tile_fn, phase, first):
        @pl.when(i >= 0)
        def _():
            if first:
                wait_pages(g, slot)
            for j in range(n_past):
                which, start = past_tile(j)
                tile_fn(which, start, j, False)
            for which, start, tile_slot in own_tiles:
                tile_fn(which, start, tile_slot, True)
            score_pages(phase)

    tile_pass(logits, 0, True)
    for w in range(2):
        mx_sc[w] = jnp.broadcast_to(jnp.max(mx_sc[w], axis=1, keepdims=True), mx_sc.shape[1:])
    score_pages(1)
    tile_pass(lambda which, start, tile_slot, own: weighted_values(which, start, tile_slot), 2, False)
    score_pages(3)

    @pl.when(g == n_steps - 1)
    def _():
        wait_pages(0, 1 - slot)

    for w, o_ref in enumerate((o_lo_ref, o_hi_ref)):
        out = None
        for hh in range(n_hh - 1, -1, -1):
            acc = acc_sc[w, hh * blk:(hh + 1) * blk, :]
            o_h = acc[:, :LANES] / acc[:, LANES:]
            out = o_h if out is None else jnp.where(lane < (hh + 1) * head_dim, o_h, out)
        o_ref[...] = out


def _moba_prompt(q, ka, vb, km, q_s, kt, pt_flat, layer, *, batch, seq, n_pages, head_dim):
    rows, da = q.shape
    assert LANES % head_dim == 0 and seq % MOBA_BLOCK == 0 and da % LANES == 0
    nb = seq // MOBA_BLOCK
    assert nb % SUBLANES == 0 and nb <= LANES
    blk = MOBA_BLOCK
    n_hh = LANES // head_dim
    half = nb // 2
    n_hp = da // LANES

    db = q_s.shape[0]
    n_heads, page = kt.shape[2], kt.shape[4]
    assert page == LANES
    n_steps = batch * n_hp * half
    total_pages = db * n_pages
    assert total_pages % n_steps == 0
    pps = total_pages // n_steps
    assert n_pages % pps == 0
    steps_per_seq = n_pages // pps
    qcol = jnp.broadcast_to(q_s.reshape(db, n_heads, head_dim, 1), (db, n_heads, head_dim, page))

    step = lambda b, hp, i: (b * n_hp + hp) * half + i
    q_spec = lambda f: pl.BlockSpec((blk, LANES), lambda b, hp, i, pt: (b * nb + f(i), hp))
    o_spec = lambda f: pl.BlockSpec((blk, LANES), lambda b, hp, i, pt: (b * half + f(i), hp))
    half_shape = jax.ShapeDtypeStruct((rows // 2, da), F32)
    lo, hi, s_all = pl.pallas_call(
        functools.partial(_moba_kernel, head_dim=head_dim, n_blocks=nb, pages_per_step=pps,
                          layer=layer),
        grid_spec=pltpu.PrefetchScalarGridSpec(
            num_scalar_prefetch=1,
            grid=(batch, n_hp, half),
            in_specs=[q_spec(lambda i: i), q_spec(lambda i: nb - 1 - i),
                      pl.BlockSpec((seq, 2 * LANES), lambda b, hp, i, pt: (b, hp)),
                      pl.BlockSpec((seq, LANES), lambda b, hp, i, pt: (b, hp)),
                      pl.BlockSpec((1, nb, LANES), lambda b, hp, i, pt: (b, 0, hp)),
                      pl.BlockSpec((1, n_heads, head_dim, page),
                                   lambda b, hp, i, pt: (step(b, hp, i) // steps_per_seq, 0, 0, 0)),
                      pl.BlockSpec(memory_space=pl.ANY)],
            out_specs=[o_spec(lambda i: i), o_spec(lambda i: half - 1 - i),
                       pl.BlockSpec((1, pps, n_heads, page),
                                    lambda b, hp, i, pt: (step(b, hp, i), 0, 0, 0))],
            scratch_shapes=[pltpu.VMEM((nb + 1, n_hh * blk, blk), F32),
                            pltpu.VMEM((2, n_hh * blk, 2 * LANES), BF16),
                            pltpu.VMEM((2, n_hh * blk, LANES), F32),
                            pltpu.VMEM((2, n_hh * blk, 2 * LANES), F32),
                            pltpu.VMEM((2, pps, n_heads, head_dim, page), F32),
                            pltpu.SemaphoreType.DMA((2,))]),
        out_shape=[half_shape, half_shape,
                   jax.ShapeDtypeStruct((n_steps, pps, n_heads, page), F32)],
        compiler_params=_cparams(("arbitrary", "arbitrary", "arbitrary")),
        name="moba_prompt",
    )(pt_flat, q, q, ka, vb, km, qcol, kt)
    return (lo, hi), s_all.reshape(db, n_pages, n_heads, page)


def _head_diag(n_heads, width, head_dim):
    sub = lax.broadcasted_iota(jnp.int32, (n_heads, width), 0)
    lane = lax.broadcasted_iota(jnp.int32, (n_heads, width), 1)
    return (lane >= sub * head_dim) & (lane < (sub + 1) * head_dim)


def _page_logits(page_buf, slot, first, last, qcol_ref, s_ref, scale):
    qcol = qcol_ref[0]
    for ii in range(first, last):
        s_ref[0, ii] = jnp.sum(page_buf[slot, ii] * qcol, axis=1) * scale


def _sample_select_kernel(s_ref, q_ref, kn_ref, p_ref, pown_ref, idx_ref, **static):
    for dd in range(s_ref.shape[0]):
        _sample_select_one(dd, s_ref, q_ref, kn_ref, p_ref, pown_ref, idx_ref, **static)


def _sample_select_one(dd, s_ref, q_ref, kn_ref, p_ref, pown_ref, idx_ref, *, ppb, n_heads, head_dim,
                       n_sel):
    da = n_heads * head_dim
    scale = head_dim ** -0.5
    s = s_ref[dd]
    n_blk = s.shape[0] // ppb
    tok = jnp.sum(s, axis=2, keepdims=True)
    sb = jnp.sum(tok.reshape(n_blk, ppb, n_heads, 1), axis=1)
    bidx = lax.broadcasted_iota(jnp.int32, sb.shape, 0)
    pblk = lax.shift_right_logical(
        lax.broadcasted_iota(jnp.int32, s.shape, 0), ppb.bit_length() - 1)
    lane = lax.broadcasted_iota(jnp.int32, (n_heads, LANES), 1)
    sel = jnp.zeros(s.shape, jnp.bool_)
    idx_out = jnp.zeros((n_heads, LANES), jnp.int32)
    for r in range(n_sel):
        pick, first = _first_max_pick(sb, bidx, 0, n_blk)
        sb = jnp.where(pick, -jnp.inf, sb)
        sel = sel | (pblk == first)
        idx_out = jnp.where(lane == r, first[0], idx_out)
    diag = _head_diag(n_heads, da, head_dim)
    qk_new = jnp.broadcast_to(q_ref[dd] * kn_ref[dd], (n_heads, da))
    s_own = jnp.sum(jnp.where(diag, qk_new, 0.0), axis=1, keepdims=True) * scale
    sm = jnp.where(sel, s, -jnp.inf)
    m = jnp.max(jnp.max(sm, axis=0), axis=1, keepdims=True)
    m = jnp.maximum(m, s_own)
    p = jnp.where(sel, jnp.exp(s - m[None]), 0.0)
    p_own = jnp.exp(s_own - m)
    den = jnp.sum(jnp.sum(p, axis=0), axis=1, keepdims=True) + p_own
    inv = 1.0 / den
    p_ref[dd] = p * inv[None]
    pown_ref[dd] = jnp.broadcast_to(p_own * inv, (n_heads, LANES))
    idx_ref[dd] = idx_out


def _sample_value_kernel(pt_ref, sel_ref, p_ref, pown_ref, vn_ref, *rest, n_sel, ppb, n_heads):
    v_refs = rest[:-1]
    o_ref = rest[-1]
    d = pl.program_id(0)
    page = p_ref.shape[-1]
    head_dim = o_ref.shape[-1]
    lane = lax.broadcasted_iota(jnp.int32, (head_dim, page), 1)
    cols = jnp.zeros((head_dim, page), F32)
    for h in range(n_heads):
        acc = None
        for r in range(n_sel):
            s = h * n_sel + r
            blk = sel_ref[d * (n_heads * n_sel) + s]
            for u in range(ppb):
                pp = p_ref[0, blk * ppb + u, h:h + 1, :]
                term = v_refs[s * ppb + u][0, 0, 0] * pp
                acc = term if acc is None else acc + term
        cols = jnp.where(lane == h, jnp.sum(acc, axis=1, keepdims=True), cols)
    if head_dim < page:
        cols = jnp.concatenate([cols, jnp.zeros((page - head_dim, page), F32)], axis=0)
    out = cols.T[:n_heads, :head_dim]
    o_ref[0] = out + pown_ref[0][:, 0:1] * vn_ref[0]


def _moba_sample(s_all, q, k_new, v_new, vt, pt_flat, layer, *, n_heads, head_dim):
    db, da = q.shape
    n_pages, page = s_all.shape[1], s_all.shape[3]
    ppb = MOBA_BLOCK // page
    assert ppb >= 1 and ppb & (ppb - 1) == 0 and MOBA_BLOCK % page == 0
    assert (n_pages * page) % MOBA_BLOCK == 0
    n_blk = n_pages // ppb
    n_sel = min(MOBA_TOPK, n_blk)
    assert n_sel > 0 and n_sel <= LANES
    q3, kn3 = (a.reshape(db, 1, da) for a in (q, k_new))
    vn3 = v_new.reshape(db, n_heads, head_dim)

    nd = _pick_tile(db, SELECT_SEQS_PER_STEP)
    row3 = pl.BlockSpec((nd, 1, da), lambda d: (d, 0, 0))
    head_spec = pl.BlockSpec((nd, n_heads, LANES), lambda d: (d, 0, 0))
    s_spec = pl.BlockSpec((nd, n_pages, n_heads, page), lambda d: (d, 0, 0, 0))
    p_all, p_own, idx = pl.pallas_call(
        functools.partial(_sample_select_kernel, ppb=ppb, n_heads=n_heads, head_dim=head_dim,
                          n_sel=n_sel),
        grid=(db // nd,),
        in_specs=[s_spec, row3, row3],
        out_specs=[s_spec, head_spec, head_spec],
        out_shape=[jax.ShapeDtypeStruct(s_all.shape, F32),
                   jax.ShapeDtypeStruct((db, n_heads, LANES), F32),
                   jax.ShapeDtypeStruct((db, n_heads, LANES), jnp.int32)],
        compiler_params=_cparams(("arbitrary",)),
        name="sample_select",
    )(s_all, q3, kn3)

    sel_flat = idx[:, :, :n_sel].reshape(-1)
    n_fetch = n_heads * n_sel * ppb
    head3 = pl.BlockSpec((1, n_heads, head_dim), lambda d, pt, sel: (d, 0, 0))

    def v_slab(d, pt, sel, s, u):
        d = jnp.minimum(d, db - 1)
        blk = jnp.clip(sel[d * (n_heads * n_sel) + s], 0, n_blk - 1)
        return (layer, pt[d * n_pages + blk * ppb + u], s // n_sel, 0, 0)

    v_specs = [pl.BlockSpec((1, 1, 1, head_dim, page),
                            functools.partial(v_slab, s=f // ppb, u=f % ppb))
               for f in range(n_fetch)]
    out = pl.pallas_call(
        functools.partial(_sample_value_kernel, n_sel=n_sel, ppb=ppb, n_heads=n_heads),
        grid_spec=pltpu.PrefetchScalarGridSpec(
            num_scalar_prefetch=2,
            grid=(db,),
            in_specs=[pl.BlockSpec((1, n_pages, n_heads, page), lambda d, pt, sel: (d, 0, 0, 0)),
                      pl.BlockSpec((1, n_heads, LANES), lambda d, pt, sel: (d, 0, 0)),
                      head3] + v_specs,
            out_specs=head3),
        out_shape=jax.ShapeDtypeStruct((db, n_heads, head_dim), F32),
        compiler_params=_cparams(("arbitrary",)),
        name="sample_value",
    )(pt_flat, sel_flat, p_all, p_own, vn3, *([vt] * n_fetch))
    return out.reshape(db, da)


def _block_diag(w):
    n, wi, wo = w.shape
    eye = jnp.eye(n, dtype=w.dtype)
    return (eye[:, None, :, None] * w[:, :, None, :]).reshape(n * wi, n * wo)


def _pick_tile(n, cap):
    t = min(n, cap)
    while n % t:
        t //= 2
    return t


def kernel(x_prompt, x_sample, cache_k, cache_v, state_rglru_h, state_conv, page_table, rms_g,
           q_norm_g, k_norm_g, w_in, conv_w, conv_b, w_a, b_a, w_x, b_x, lru_lambda, w_out):
    depth = w_in.shape[0]
    batch, seq, d = x_prompt.shape
    db, dec_seq, _ = x_sample.shape
    _, n_pool, page, n_heads, head_dim = cache_k.shape
    da = n_heads * head_dim
    dr = state_rglru_h.shape[-1]
    n_pages = page_table.shape[1]
    past_len = n_pages * page
    assert dec_seq == 1, "sample group is one new token per sequence"
    assert w_in.shape[2] == 4 * da + 2 * dr and da % LANES == 0 and dr % LANES == 0

    tm = _pick_tile(seq, 512)
    tms = _pick_tile(db, 512)
    pos_p = jnp.arange(seq)
    pos_s = jnp.full((db,), past_len, jnp.int32)
    bd = _block_diag(jnp.ones((LANES // head_dim, head_dim, head_dim), BF16))
    cache_kt = jnp.transpose(cache_k, (0, 1, 3, 4, 2))
    cache_vt = jnp.transpose(cache_v, (0, 1, 3, 4, 2))
    pt_flat = page_table.reshape(-1).astype(jnp.int32)

    yp = x_prompt.reshape(batch * seq, d)
    ys = x_sample.reshape(db, d)
    outs = {n: [] for n in ("kp", "vp", "hp", "cp", "ks", "vs", "hs", "cs")}
    for l in range(depth):
        w_in_bf = w_in[l].astype(BF16)
        w_out_bf = w_out[l].astype(BF16)
        wax = jnp.concatenate([_block_diag(w_a[l]), _block_diag(w_x[l])], axis=1).astype(BF16)
        proj = functools.partial(_proj_in, rms_g=rms_g[l], gq=q_norm_g[l], gk=k_norm_g[l],
                                 w_bf=w_in_bf, bd=bd, da=da, dr=dr, head_dim=head_dim)
        rnn_w = (conv_w[l], conv_b[l], wax, b_a[l], b_x[l], lru_lambda[l])

        q, k, v, ga, xr, gr, ka, vb, km = proj(yp, pos_p, tm=tm, with_prompt_outs=True)
        q_s, k_s, v_s, ga_s, xr_s, gr_s = proj(ys, pos_s, tm=tms, with_prompt_outs=False)

        km = km[:, :tm // MOBA_BLOCK, :].reshape(batch, seq // MOBA_BLOCK, da)
        attn, s_all = _moba_prompt(q, ka, vb, km, q_s, cache_kt, pt_flat, l, batch=batch, seq=seq,
                                   n_pages=n_pages, head_dim=head_dim)
        rnn, h_last, cbuf = _rnn_prompt(xr, gr, *rnn_w, batch=batch, seq=seq, tr=tm)
        yp = _proj_out(yp, attn, ga, rnn, w_out_bf, tm=tm, tiles_per_seq=seq // tm)
        to_out = lambda t: t.reshape(batch, n_heads, head_dim, seq).transpose(0, 3, 1, 2)
        outs["kp"].append(to_out(k))
        outs["vp"].append(to_out(v))
        outs["hp"].append(h_last.reshape(batch, dr))
        outs["cp"].append(cbuf)

        q, k, v, ga, xr, gr = q_s, k_s, v_s, ga_s, xr_s, gr_s
        attn = _moba_sample(s_all, q, k, v, cache_vt, pt_flat, l, n_heads=n_heads, head_dim=head_dim)
        cs = state_conv[l].reshape(db, -1)
        rnn, h_last, cs_new = _rnn_sample(xr, gr, cs, state_rglru_h[l], *rnn_w)
        ys = _proj_out(ys, attn, ga, rnn, w_out_bf, tm=tms)
        outs["ks"].append(k.reshape(db, 1, n_heads, head_dim))
        outs["vs"].append(v.reshape(db, 1, n_heads, head_dim))
        outs["hs"].append(h_last)
        outs["cs"].append(cs_new.reshape(db, -1, dr))

    st = lambda n: jnp.stack(outs[n])
    return (yp.reshape(batch, seq, d), ys.reshape(db, 1, d), st("kp"), st("vp"), st("hp"),
            st("cp"), st("ks"), st("vs"), st("hs"), st("cs"))
```

```python
import functools

import jax
import jax.numpy as jnp
from jax import lax
from jax.experimental import pallas as pl
from jax.experimental.pallas import tpu as pltpu

ROPE_THETA = 500000.0
ROPE_FRACTION = 4
MOBA_BLOCK = 256
MOBA_TOPK = 3
LRU_C = 8.0
EPS = 1e-6
NEG = -1e30
LOG2E = 1.4426950408889634
PAGE_PHASE_CUTS = (0.0, 0.4375, 0.5, 0.9375, 1.0)
SELECT_SEQS_PER_STEP = 8

LANES = 128
SUBLANES = 8
VMEM_LIMIT_BYTES = 56 * 1024 * 1024

F32 = jnp.float32
BF16 = jnp.bfloat16
NT_DIMS = (((1,), (1,)), ((), ()))


def _cparams(sem):
    return pltpu.CompilerParams(dimension_semantics=sem, vmem_limit_bytes=VMEM_LIMIT_BYTES)


def _sigmoid(x):
    return 0.5 * jnp.tanh(0.5 * x) + 0.5


def _silu(x):
    return x * _sigmoid(x)


def _proj_in_kernel(x_ref, g_ref, w_ref, gq_ref, gk_ref, bd_ref, c_ref, s1_ref, s2_ref,
                    q_ref, k_ref, v_ref, ga_ref, xr_ref, gr_ref, *rest,
                    da, dr, head_dim, pos_tiles, with_prompt_outs):
    x = x_ref[...]
    ms = jnp.mean(x * x, axis=-1, keepdims=True)
    h = (x * lax.rsqrt(ms + EPS) * g_ref[...]).astype(BF16)

    cosf, sin_up, sin_dn = c_ref[...], s1_ref[...], s2_ref[...]
    bd = bd_ref[...]
    half = head_dim // ROPE_FRACTION // 2

    def proj(start, width):
        return jnp.dot(h, w_ref[:, start:start + width], preferred_element_type=F32)

    def head_norm_rope(t, gain):
        sq = t * t
        hi = sq.astype(BF16)
        lo = (sq - hi.astype(F32)).astype(BF16)
        cols = []
        for c in range(da // LANES):
            grp = slice(c * LANES, (c + 1) * LANES)
            ssum = (jnp.dot(hi[:, grp], bd, preferred_element_type=F32)
                    + jnp.dot(lo[:, grp], bd, preferred_element_type=F32))
            tc = t[:, grp] * lax.rsqrt(ssum * (1.0 / head_dim) + EPS) * gain[:, grp]
            up = pltpu.roll(tc, LANES - half, 1)
            dn = pltpu.roll(tc, half, 1)
            cols.append(tc * cosf + up * sin_up + dn * sin_dn)
        return jnp.concatenate(cols, axis=1)

    q_ref[...] = head_norm_rope(proj(0, da), gq_ref[...])
    k = head_norm_rope(proj(da, da), gk_ref[...])
    v = proj(2 * da, da)
    if with_prompt_outs:
        k_ref[0] = k.T
        v_ref[0] = v.T
    else:
        k_ref[...] = k
        v_ref[...] = v
    ga_ref[...] = proj(3 * da, da)
    xr_ref[...] = proj(4 * da, dr)
    gr_ref[...] = proj(4 * da + dr, dr)

    if with_prompt_outs:
        ka_ref, vb_ref, km_ref = rest
        tm = k.shape[0]
        vb_ref[...] = v.astype(BF16)
        lane = lax.broadcasted_iota(jnp.int32, (tm, LANES), 1)
        row_blk = lax.shift_right_logical(lax.broadcasted_iota(jnp.int32, (tm, LANES), 0),
                                          MOBA_BLOCK.bit_length() - 1)
        blk_id = (pl.program_id(0) % pos_tiles) * (tm // MOBA_BLOCK) + row_blk
        onehot = jnp.where(lane == blk_id, 1.0, 0.0).astype(BF16)
        kb = k.astype(BF16)
        ka_ref[...] = jnp.concatenate(
            [piece for c in range(da // LANES) for piece in (kb[:, c * LANES:(c + 1) * LANES], onehot)],
            axis=1)
        rows = [jnp.mean(k[b * MOBA_BLOCK:(b + 1) * MOBA_BLOCK], axis=0, keepdims=True)
                for b in range(tm // MOBA_BLOCK)]
        rows.append(jnp.zeros((SUBLANES - len(rows), da), F32))
        km_ref[0] = jnp.concatenate(rows, axis=0)


def _rope_tables(pos, head_dim):
    rope_dim = head_dim // ROPE_FRACTION
    half = rope_dim // 2
    inv = ROPE_THETA ** (-jnp.arange(half, dtype=F32) * 2.0 / rope_dim)
    ang = pos.astype(F32)[:, None] * inv
    cos, sin = jnp.cos(ang), jnp.sin(ang)
    n = pos.shape[0]
    ones = jnp.ones((n, head_dim - rope_dim), F32)
    zeros = jnp.zeros((n, head_dim - rope_dim), F32)
    zh = jnp.zeros((n, half), F32)
    reps = LANES // head_dim
    cosf = jnp.tile(jnp.concatenate([cos, cos, ones], axis=1), (1, reps))
    sin_up = jnp.tile(jnp.concatenate([-sin, zh, zeros], axis=1), (1, reps))
    sin_dn = jnp.tile(jnp.concatenate([zh, sin, zeros], axis=1), (1, reps))
    return cosf, sin_up, sin_dn


def _proj_in(x2d, pos, rms_g, gq, gk, w_bf, bd, *, da, dr, head_dim, tm, with_prompt_outs):
    rows, d = x2d.shape
    n_pos = pos.shape[0]
    assert rows % tm == 0 and n_pos % tm == 0
    pos_tiles = n_pos // tm
    cosf, sin_up, sin_dn = _rope_tables(pos, head_dim)
    reps = da // head_dim

    row_spec = lambda w: pl.BlockSpec((tm, w), lambda i: (i, 0))
    full = lambda a: pl.BlockSpec(a.shape, lambda i: (0,) * a.ndim)
    tab_spec = pl.BlockSpec((tm, LANES), lambda i: (i % pos_tiles, 0))

    g2 = rms_g.reshape(1, d)
    gq2 = jnp.tile(gq, reps).reshape(1, da)
    gk2 = jnp.tile(gk, reps).reshape(1, da)

    out_shape = [jax.ShapeDtypeStruct((rows, da), F32)] * 4 + [jax.ShapeDtypeStruct((rows, dr), F32)] * 2
    out_specs = [row_spec(da)] * 4 + [row_spec(dr)] * 2
    if with_prompt_outs:
        out_shape[1:3] = [jax.ShapeDtypeStruct((rows // n_pos, da, n_pos), F32)] * 2
        out_specs[1:3] = [pl.BlockSpec((1, da, tm), lambda i: (i // pos_tiles, 0, i % pos_tiles))] * 2
        assert tm % MOBA_BLOCK == 0 and tm // MOBA_BLOCK <= SUBLANES
        assert n_pos // MOBA_BLOCK <= LANES and MOBA_BLOCK & (MOBA_BLOCK - 1) == 0
        out_shape += [jax.ShapeDtypeStruct((rows, 2 * da), BF16)]
        out_shape += [jax.ShapeDtypeStruct((rows, da), BF16)]
        out_shape += [jax.ShapeDtypeStruct((rows // tm, SUBLANES, da), F32)]
        out_specs += [row_spec(2 * da), row_spec(da)]
        out_specs += [pl.BlockSpec((1, SUBLANES, da), lambda i: (i, 0, 0))]

    kern = functools.partial(_proj_in_kernel, da=da, dr=dr, head_dim=head_dim, pos_tiles=pos_tiles,
                             with_prompt_outs=with_prompt_outs)
    return pl.pallas_call(
        kern,
        grid=(rows // tm,),
        in_specs=[row_spec(d), full(g2), full(w_bf), full(gq2), full(gk2), full(bd),
                  tab_spec, tab_spec, tab_spec],
        out_specs=out_specs,
        out_shape=out_shape,
        compiler_params=_cparams(("arbitrary",)),
        name="proj_in",
    )(x2d, g2, w_bf, gq2, gk2, bd, cosf, sin_up, sin_dn)


def _proj_out_kernel(x_ref, *refs, da, half_tiles):
    if half_tiles:
        lo_ref, hi_ref, ga_ref, rnn_ref, w_ref, y_ref = refs
        in_low_half = pl.program_id(0) % (2 * half_tiles) < half_tiles
        attn = jnp.where(in_low_half, lo_ref[...], hi_ref[...])
    else:
        attn_ref, ga_ref, rnn_ref, w_ref, y_ref = refs
        attn = attn_ref[...]
    a = (attn * _silu(ga_ref[...])).astype(BF16)
    r = rnn_ref[...].astype(BF16)
    y = jnp.dot(a, w_ref[0:da, :], preferred_element_type=F32)
    y = y + jnp.dot(r, w_ref[da:, :], preferred_element_type=F32)
    y_ref[...] = x_ref[...] + y


def _proj_out(x2d, attn, ga, rnn, w_bf, *, tm, tiles_per_seq=None):
    rows, d = x2d.shape
    da, dr = ga.shape[1], rnn.shape[1]
    assert rows % tm == 0
    row_spec = lambda w: pl.BlockSpec((tm, w), lambda i: (i, 0))
    if isinstance(attn, tuple):
        assert tiles_per_seq % 2 == 0
        half_tiles = tiles_per_seq // 2
        seq_base = lambda i: (i // tiles_per_seq) * half_tiles
        attn_specs = [
            pl.BlockSpec((tm, da), lambda i: (seq_base(i) + jnp.minimum(i % tiles_per_seq,
                                                                        half_tiles - 1), 0)),
            pl.BlockSpec((tm, da), lambda i: (seq_base(i) + jnp.maximum(i % tiles_per_seq
                                                                        - half_tiles, 0), 0))]
        attn_args = attn
    else:
        half_tiles, attn_specs, attn_args = 0, [row_spec(da)], (attn,)
    return pl.pallas_call(
        functools.partial(_proj_out_kernel, da=da, half_tiles=half_tiles),
        grid=(rows // tm,),
        in_specs=[row_spec(d)] + attn_specs + [row_spec(da), row_spec(dr),
                                               pl.BlockSpec(w_bf.shape, lambda i: (0, 0))],
        out_specs=row_spec(d),
        out_shape=jax.ShapeDtypeStruct((rows, d), F32),
        compiler_params=_cparams(("arbitrary",)),
        name="proj_out",
    )(x2d, *attn_args, ga, rnn, w_bf)


def _lru_coeffs(xc, wax_ref, ba_ref, bx_ref, lam_ref, dr):
    pre = jnp.dot(xc.astype(BF16), wax_ref[...], preferred_element_type=F32)
    r = _sigmoid(pre[:, :dr] + ba_ref[...])
    gi = _sigmoid(pre[:, dr:] + bx_ref[...])
    z = -lam_ref[...]
    softplus = jnp.maximum(z, 0.0) + jnp.log1p(jnp.exp(-jnp.abs(z)))
    log_a = -LRU_C * r * softplus
    a = jnp.exp(log_a)
    gap = 1.0 - a * a
    mult = jnp.where(gap > 0.0, gap * lax.rsqrt(gap), 0.0)
    return a, mult * (gi * xc)


def _rnn_prompt_kernel(xr_ref, gr_ref, cw_ref, cb_ref, wax_ref, ba_ref, bx_ref, lam_ref,
                       out_ref, hl_ref, cbuf_ref, xe_ref, hc_ref, *, tr, dr, conv_w):
    t = pl.program_id(1)
    pad = SUBLANES

    @pl.when(t == 0)
    def _():
        xe_ref[0:pad, :] = jnp.zeros((pad, dr), F32)
        hc_ref[...] = jnp.zeros_like(hc_ref)

    x = xr_ref[...]
    xe_ref[pad:pad + tr, :] = x
    y = cb_ref[...] + cw_ref[conv_w - 1:conv_w, :] * x
    for j in range(conv_w - 1):
        back = conv_w - 1 - j
        y = y + cw_ref[j:j + 1, :] * xe_ref[pad - back:pad - back + tr, :]
    cbuf_ref[0] = xe_ref[pad + tr - (conv_w - 1):pad + tr, :]
    xe_ref[0:pad, :] = xe_ref[tr:tr + pad, :]

    a, b = _lru_coeffs(y, wax_ref, ba_ref, bx_ref, lam_ref, dr)

    grp = SUBLANES
    a = a.reshape(tr // grp, grp, dr)
    b = b.reshape(tr // grp, grp, dr)
    sub = lax.broadcasted_iota(jnp.int32, a.shape, 1)
    s = 1
    while s < grp:
        keep = sub >= s
        a_sh = pltpu.roll(a, s, 1)
        b_sh = pltpu.roll(b, s, 1)
        b = jnp.where(keep, a * b_sh + b, b)
        a = jnp.where(keep, a * a_sh, a)
        s *= 2
    h_prev = hc_ref[...]
    h_groups = []
    for g in range(tr // grp):
        h_g = a[g] * h_prev + b[g]
        h_groups.append(h_g)
        h_prev = h_g[grp - 1:grp, :]
    h = jnp.concatenate(h_groups, axis=0)
    h_last = h_prev
    hc_ref[...] = h_last
    hl_ref[0] = h_last
    out_ref[...] = (h * _silu(gr_ref[...])).astype(BF16)


def _rnn_prompt(xr, gr, conv_w, conv_b, wax, b_a, b_x, lam, *, batch, seq, tr):
    rows, dr = xr.shape
    cw = conv_w.shape[0]
    assert seq % tr == 0 and rows == batch * seq and cw - 1 <= SUBLANES
    nt = seq // tr
    row_spec = pl.BlockSpec((tr, dr), lambda b, t: (b * nt + t, 0))
    full = lambda a: pl.BlockSpec(a.shape, lambda b, t: (0,) * a.ndim)
    vec = lambda a: a.reshape(1, dr)
    args = (xr, gr, conv_w, vec(conv_b), wax, vec(b_a), vec(b_x), vec(lam))
    return pl.pallas_call(
        functools.partial(_rnn_prompt_kernel, tr=tr, dr=dr, conv_w=cw),
        grid=(batch, nt),
        in_specs=[row_spec, row_spec] + [full(a) for a in args[2:]],
        out_specs=[row_spec,
                   pl.BlockSpec((1, 1, dr), lambda b, t: (b, 0, 0)),
                   pl.BlockSpec((1, cw - 1, dr), lambda b, t: (b, 0, 0))],
        out_shape=[jax.ShapeDtypeStruct((rows, dr), BF16),
                   jax.ShapeDtypeStruct((batch, 1, dr), F32),
                   jax.ShapeDtypeStruct((batch, cw - 1, dr), F32)],
        scratch_shapes=[pltpu.VMEM((SUBLANES + tr, dr), F32), pltpu.VMEM((1, dr), F32)],
        compiler_params=_cparams(("arbitrary", "arbitrary")),
        name="rnn_prompt",
    )(*args)


def _rnn_sample_kernel(xr_ref, gr_ref, cs_ref, h0_ref, cw_ref, cb_ref, wax_ref, ba_ref, bx_ref,
                       lam_ref, out_ref, hl_ref, cs_out_ref, *, dr, conv_w):
    x = xr_ref[...]
    y = cb_ref[...] + cw_ref[conv_w - 1:conv_w, :] * x
    for j in range(conv_w - 1):
        y = y + cw_ref[j:j + 1, :] * cs_ref[:, j * dr:(j + 1) * dr]
    a, b = _lru_coeffs(y, wax_ref, ba_ref, bx_ref, lam_ref, dr)
    h = a * h0_ref[...] + b
    hl_ref[...] = h
    out_ref[...] = (h * _silu(gr_ref[...])).astype(BF16)
    if conv_w > 2:
        cs_out_ref[:, 0:(conv_w - 2) * dr] = cs_ref[:, dr:(conv_w - 1) * dr]
    cs_out_ref[:, (conv_w - 2) * dr:] = x


def _rnn_sample(xr, gr, cs, h0, conv_w, conv_b, wax, b_a, b_x, lam):
    rows, dr = xr.shape
    cw = conv_w.shape[0]
    vec = lambda a: a.reshape(1, dr)
    args = (xr, gr, cs, h0, conv_w, vec(conv_b), wax, vec(b_a), vec(b_x), vec(lam))
    full = lambda a: pl.BlockSpec(a.shape, lambda i: (0,) * a.ndim)
    return pl.pallas_call(
        functools.partial(_rnn_sample_kernel, dr=dr, conv_w=cw),
        grid=(1,),
        in_specs=[full(a) for a in args],
        out_specs=[full(xr), full(xr), full(cs)],
        out_shape=[jax.ShapeDtypeStruct((rows, dr), BF16), jax.ShapeDtypeStruct((rows, dr), F32),
                   jax.ShapeDtypeStruct(cs.shape, F32)],
        compiler_params=_cparams(("arbitrary",)),
        name="rnn_sample",
    )(*args)


def _first_max_pick(s, idx, axis, sentinel):
    m = jnp.max(s, axis=axis, keepdims=True)
    first = jnp.min(jnp.where((s == m) & (m > -jnp.inf), idx, sentinel), axis=axis, keepdims=True)
    return idx == first, first


def _bf16_parts(x):
    hi = x.astype(BF16)
    return hi, (x - hi.astype(F32)).astype(BF16)


def _moba_kernel(pt_ref, q_lo_ref, q_hi_ref, ka_ref, vb_ref, km_ref, qcol_ref, kt_hbm,
                 o_lo_ref, o_hi_ref, s_page_ref, s_sc, qa_sc, mx_sc, acc_sc, page_buf, page_sem, *,
                 head_dim, n_blocks, pages_per_step, layer):
    i = pl.program_id(2)
    g = (pl.program_id(0) * pl.num_programs(1) + pl.program_id(1)) * pl.num_programs(2) + i
    n_steps = pl.num_programs(0) * pl.num_programs(1) * pl.num_programs(2)
    slot = g % 2

    def page_copy(step, buf_slot, ii):
        return pltpu.make_async_copy(kt_hbm.at[layer, pt_ref[step * pages_per_step + ii]],
                                     page_buf.at[buf_slot, ii], page_sem.at[buf_slot])

    def start_pages(step, buf_slot):
        for ii in range(pages_per_step):
            page_copy(step, buf_slot, ii).start()

    def wait_pages(step, buf_slot):
        for ii in range(pages_per_step):
            page_copy(step, buf_slot, ii).wait()

    @pl.when(g == 0)
    def _():
        start_pages(0, 0)

    start_pages(jnp.where(g + 1 < n_steps, g + 1, 0), 1 - slot)

    cuts =[round(pages_per_step * c) for c in PAGE_PHASE_CUTS]

    def score_pages(phase):
        _page_logits(page_buf, slot, cuts[phase], cuts[phase + 1], qcol_ref, s_page_ref,
                     head_dim ** -0.5)

    blk = MOBA_BLOCK
    qscale = head_dim ** -0.5 * LOG2E
    n_past = n_blocks - 1
    lane = lax.broadcasted_iota(jnp.int32, (blk, LANES), 1)
    n_hh = LANES // head_dim
    future = (lax.broadcasted_iota(jnp.int32, (blk, blk), 1)
              > lax.broadcasted_iota(jnp.int32, (blk, blk), 0))
    brow = lax.broadcasted_iota(jnp.int32, (n_blocks, blk), 0)
    km = km_ref[0]
    q_refs = (q_lo_ref, q_hi_ref)
    qblks = (i, n_blocks - 1 - i)

    def past_tile(j):
        second = j >= i
        start = pl.multiple_of(jnp.where(second, j - i, j) * blk, blk)
        return second.astype(jnp.int32), start

    km_parts = _bf16_parts(km)
    for hh in range(n_hh):
        in_head = (lane >= hh * head_dim) & (lane < (hh + 1) * head_dim)
        for w in range(2):
            qm = jnp.where(in_head, q_refs[w][...], 0.0)
            q_parts = _bf16_parts(qm)
            sb = sum(lax.dot_general(km_parts[a], q_parts[b], NT_DIMS, preferred_element_type=F32)
                     for a, b in ((0, 0), (0, 1), (1, 0)))
            sb = jnp.where(brow < qblks[w], sb, -jnp.inf)
            sel = brow == qblks[w]
            for _ in range(min(MOBA_TOPK, n_blocks)):
                pick, _ = _first_max_pick(sb, brow, 0, n_blocks)
                sel = sel | pick
                sb = jnp.where(pick, -jnp.inf, sb)
            bias_t = jnp.where(sel, 0.0, NEG)
            bias_t = jnp.concatenate([bias_t, jnp.full((LANES - n_blocks, blk), NEG, F32)], axis=0)
            qa_sc[w, hh * blk:(hh + 1) * blk, :] = jnp.concatenate(
                [(qm * qscale).astype(BF16), bias_t.T.astype(BF16)], axis=1)
    mx_sc[...] = jnp.full(mx_sc.shape, NEG, F32)
    acc_sc[...] = jnp.zeros(acc_sc.shape, F32)
    ones = jnp.ones((blk, LANES), BF16)

    def logits(which, start, slot, own):
        ka = ka_ref[pl.ds(start, blk), :]
        for hh in range(n_hh):
            rows = pl.ds(hh * blk, blk)
            s = lax.dot_general(qa_sc[which, rows, :], ka, NT_DIMS, preferred_element_type=F32)
            if own:
                s = jnp.where(future, NEG, s)
            s_sc[slot, rows, :] = s
            mx_sc[which, rows, :] = jnp.maximum(mx_sc[which, rows, :],
                                                jnp.maximum(s[:, :LANES], s[:, LANES:]))

    def weighted_values(which, start, slot):
        va = jnp.concatenate([vb_ref[pl.ds(start, blk), :], ones], axis=1)
        for hh in range(n_hh):
            rows = pl.ds(hh * blk, blk)
            m = mx_sc[which, rows, :]
            s = s_sc[slot, rows, :]
            p = jnp.concatenate([jnp.exp2(s[:, :LANES] - m), jnp.exp2(s[:, LANES:] - m)], axis=1)
            acc_sc[which, rows, :] = acc_sc[which, rows, :] + jnp.dot(
                p.astype(BF16), va, preferred_element_type=F32)

    own_tiles = [(w, pl.multiple_of(qblks[w] * blk, blk), n_past + w) for w in range(2)]

    def tile_pass(tile_fn, phase, first):
        @pl.when(i >= 0)
        def _():
            if first:
                wait_pages(g, slot)
            for j in range(n_past):
                which, start = past_tile(j)
                tile_fn(which, start, j, False)
            for which, start, tile_slot in own_tiles:
                tile_fn(which, start, tile_slot, True)
            score_pages(phase)

    tile_pass(logits, 0, True)
    for w in range(2):
        mx_sc[w] = jnp.broadcast_to(jnp.max(mx_sc[w], axis=1, keepdims=True), mx_sc.shape[1:])
    score_pages(1)
    tile_pass(lambda which, start, tile_slot, own: weighted_values(which, start, tile_slot), 2, False)
    score_pages(3)

    @pl.when(g == n_steps - 1)
    def _():
        wait_pages(0, 1 - slot)

    for w, o_ref in enumerate((o_lo_ref, o_hi_ref)):
        out = None
        for hh in range(n_hh - 1, -1, -1):
            acc = acc_sc[w, hh * blk:(hh + 1) * blk, :]
            o_h = acc[:, :LANES] / acc[:, LANES:]
            out = o_h if out is None else jnp.where(lane < (hh + 1) * head_dim, o_h, out)
        o_ref[...] = out


def _moba_prompt(q, ka, vb, km, q_s, kt, pt_flat, layer, *, batch, seq, n_pages, head_dim):
    rows, da = q.shape
    assert LANES % head_dim == 0 and seq % MOBA_BLOCK == 0 and da % LANES == 0
    nb = seq // MOBA_BLOCK
    assert nb % SUBLANES == 0 and nb <= LANES
    blk = MOBA_BLOCK
    n_hh = LANES // head_dim
    half = nb // 2
    n_hp = da // LANES

    db = q_s.shape[0]
    n_heads, page = kt.shape[2], kt.shape[4]
    assert page == LANES
    n_steps = batch * n_hp * half
    total_pages = db * n_pages
    assert total_pages % n_steps == 0
    pps = total_pages // n_steps
    assert n_pages % pps == 0
    steps_per_seq = n_pages // pps
    qcol = jnp.broadcast_to(q_s.reshape(db, n_heads, head_dim, 1), (db, n_heads, head_dim, page))

    step = lambda b, hp, i: (b * n_hp + hp) * half + i
    q_spec = lambda f: pl.BlockSpec((blk, LANES), lambda b, hp, i, pt: (b * nb + f(i), hp))
    o_spec = lambda f: pl.BlockSpec((blk, LANES), lambda b, hp, i, pt: (b * half + f(i), hp))
    half_shape = jax.ShapeDtypeStruct((rows // 2, da), F32)
    lo, hi, s_all = pl.pallas_call(
        functools.partial(_moba_kernel, head_dim=head_dim, n_blocks=nb, pages_per_step=pps,
                          layer=layer),
        grid_spec=pltpu.PrefetchScalarGridSpec(
            num_scalar_prefetch=1,
            grid=(batch, n_hp, half),
            in_specs=[q_spec(lambda i: i), q_spec(lambda i: nb - 1 - i),
                      pl.BlockSpec((seq, 2 * LANES), lambda b, hp, i, pt: (b, hp)),
                      pl.BlockSpec((seq, LANES), lambda b, hp, i, pt: (b, hp)),
                      pl.BlockSpec((1, nb, LANES), lambda b, hp, i, pt: (b, 0, hp)),
                      pl.BlockSpec((1, n_heads, head_dim, page),
                                   lambda b, hp, i, pt: (step(b, hp, i) // steps_per_seq, 0, 0, 0)),
                      pl.BlockSpec(memory_space=pl.ANY)],
            out_specs=[o_spec(lambda i: i), o_spec(lambda i: half - 1 - i),
                       pl.BlockSpec((1, pps, n_heads, page),
                                    lambda b, hp, i, pt: (step(b, hp, i), 0, 0, 0))],
            scratch_shapes=[pltpu.VMEM((nb + 1, n_hh * blk, blk), F32),
                            pltpu.VMEM((2, n_hh * blk, 2 * LANES), BF16),
                            pltpu.VMEM((2, n_hh * blk, LANES), F32),
                            pltpu.VMEM((2, n_hh * blk, 2 * LANES), F32),
                            pltpu.VMEM((2, pps, n_heads, head_dim, page), F32),
                            pltpu.SemaphoreType.DMA((2,))]),
        out_shape=[half_shape, half_shape,
                   jax.ShapeDtypeStruct((n_steps, pps, n_heads, page), F32)],
        compiler_params=_cparams(("arbitrary", "arbitrary", "arbitrary")),
        name="moba_prompt",
    )(pt_flat, q, q, ka, vb, km, qcol, kt)
    return (lo, hi), s_all.reshape(db, n_pages, n_heads, page)


def _head_diag(n_heads, width, head_dim):
    sub = lax.broadcasted_iota(jnp.int32, (n_heads, width), 0)
    lane = lax.broadcasted_iota(jnp.int32, (n_heads, width), 1)
    return (lane >= sub * head_dim) & (lane < (sub + 1) * head_dim)


def _page_logits(page_buf, slot, first, last, qcol_ref, s_ref, scale):
    qcol = qcol_ref[0]
    for ii in range(first, last):
        s_ref[0, ii] = jnp.sum(page_buf[slot, ii] * qcol, axis=1) * scale


def _sample_select_kernel(s_ref, q_ref, kn_ref, p_ref, pown_ref, idx_ref, **static):
    for dd in range(s_ref.shape[0]):
        _sample_select_one(dd, s_ref, q_ref, kn_ref, p_ref, pown_ref, idx_ref, **static)


def _sample_select_one(dd, s_ref, q_ref, kn_ref, p_ref, pown_ref, idx_ref, *, ppb, n_heads, head_dim,
                       n_sel):
    da = n_heads * head_dim
    scale = head_dim ** -0.5
    s = s_ref[dd]
    n_blk = s.shape[0] // ppb
    tok = jnp.sum(s, axis=2, keepdims=True)
    sb = jnp.sum(tok.reshape(n_blk, ppb, n_heads, 1), axis=1)
    bidx = lax.broadcasted_iota(jnp.int32, sb.shape, 0)
    pblk = lax.shift_right_logical(
        lax.broadcasted_iota(jnp.int32, s.shape, 0), ppb.bit_length() - 1)
    lane = lax.broadcasted_iota(jnp.int32, (n_heads, LANES), 1)
    sel = jnp.zeros(s.shape, jnp.bool_)
    idx_out = jnp.zeros((n_heads, LANES), jnp.int32)
    for r in range(n_sel):
        pick, first = _first_max_pick(sb, bidx, 0, n_blk)
        sb = jnp.where(pick, -jnp.inf, sb)
        sel = sel | (pblk == first)
        idx_out = jnp.where(lane == r, first[0], idx_out)
    diag = _head_diag(n_heads, da, head_dim)
    qk_new = jnp.broadcast_to(q_ref[dd] * kn_ref[dd], (n_heads, da))
    s_own = jnp.sum(jnp.where(diag, qk_new, 0.0), axis=1, keepdims=True) * scale
    sm = jnp.where(sel, s, -jnp.inf)
    m = jnp.max(jnp.max(sm, axis=0), axis=1, keepdims=True)
    m = jnp.maximum(m, s_own)
    p = jnp.where(sel, jnp.exp(s - m[None]), 0.0)
    p_own = jnp.exp(s_own - m)
    den = jnp.sum(jnp.sum(p, axis=0), axis=1, keepdims=True) + p_own
    inv = 1.0 / den
    p_ref[dd] = p * inv[None]
    pown_ref[dd] = jnp.broadcast_to(p_own * inv, (n_heads, LANES))
    idx_ref[dd] = idx_out


def _sample_value_kernel(pt_ref, sel_ref, p_ref, pown_ref, vn_ref, *rest, n_sel, ppb, n_heads):
    v_refs = rest[:-1]
    o_ref = rest[-1]
    d = pl.program_id(0)
    page = p_ref.shape[-1]
    head_dim = o_ref.shape[-1]
    lane = lax.broadcasted_iota(jnp.int32, (head_dim, page), 1)
    cols = jnp.zeros((head_dim, page), F32)
    for h in range(n_heads):
        acc = None
        for r in range(n_sel):
            s = h * n_sel + r
            blk = sel_ref[d * (n_heads * n_sel) + s]
            for u in range(ppb):
                pp = p_ref[0, blk * ppb + u, h:h + 1, :]
                term = v_refs[s * ppb + u][0, 0, 0] * pp
                acc = term if acc is None else acc + term
        cols = jnp.where(lane == h, jnp.sum(acc, axis=1, keepdims=True), cols)
    if head_dim < page:
        cols = jnp.concatenate([cols, jnp.zeros((page - head_dim, page), F32)], axis=0)
    out = cols.T[:n_heads, :head_dim]
    o_ref[0] = out + pown_ref[0][:, 0:1] * vn_ref[0]


def _moba_sample(s_all, q, k_new, v_new, vt, pt_flat, layer, *, n_heads, head_dim):
    db, da = q.shape
    n_pages, page = s_all.shape[1], s_all.shape[3]
    ppb = MOBA_BLOCK // page
    assert ppb >= 1 and ppb & (ppb - 1) == 0 and MOBA_BLOCK % page == 0
    assert (n_pages * page) % MOBA_BLOCK == 0
    n_blk = n_pages // ppb
    n_sel = min(MOBA_TOPK, n_blk)
    assert n_sel > 0 and n_sel <= LANES
    q3, kn3 = (a.reshape(db, 1, da) for a in (q, k_new))
    vn3 = v_new.reshape(db, n_heads, head_dim)

    nd = _pick_tile(db, SELECT_SEQS_PER_STEP)
    row3 = pl.BlockSpec((nd, 1, da), lambda d: (d, 0, 0))
    head_spec = pl.BlockSpec((nd, n_heads, LANES), lambda d: (d, 0, 0))
    s_spec = pl.BlockSpec((nd, n_pages, n_heads, page), lambda d: (d, 0, 0, 0))
    p_all, p_own, idx = pl.pallas_call(
        functools.partial(_sample_select_kernel, ppb=ppb, n_heads=n_heads, head_dim=head_dim,
                          n_sel=n_sel),
        grid=(db // nd,),
        in_specs=[s_spec, row3, row3],
        out_specs=[s_spec, head_spec, head_spec],
        out_shape=[jax.ShapeDtypeStruct(s_all.shape, F32),
                   jax.ShapeDtypeStruct((db, n_heads, LANES), F32),
                   jax.ShapeDtypeStruct((db, n_heads, LANES), jnp.int32)],
        compiler_params=_cparams(("arbitrary",)),
        name="sample_select",
    )(s_all, q3, kn3)

    sel_flat = idx[:, :, :n_sel].reshape(-1)
    n_fetch = n_heads * n_sel * ppb
    head3 = pl.BlockSpec((1, n_heads, head_dim), lambda d, pt, sel: (d, 0, 0))

    def v_slab(d, pt, sel, s, u):
        d = jnp.minimum(d, db - 1)
        blk = jnp.clip(sel[d * (n_heads * n_sel) + s], 0, n_blk - 1)
        return (layer, pt[d * n_pages + blk * ppb + u], s // n_sel, 0, 0)

    v_specs = [pl.BlockSpec((1, 1, 1, head_dim, page),
                            functools.partial(v_slab, s=f // ppb, u=f % ppb))
               for f in range(n_fetch)]
    out = pl.pallas_call(
        functools.partial(_sample_value_kernel, n_sel=n_sel, ppb=ppb, n_heads=n_heads),
        grid_spec=pltpu.PrefetchScalarGridSpec(
            num_scalar_prefetch=2,
            grid=(db,),
            in_specs=[pl.BlockSpec((1, n_pages, n_heads, page), lambda d, pt, sel: (d, 0, 0, 0)),
                      pl.BlockSpec((1, n_heads, LANES), lambda d, pt, sel: (d, 0, 0)),
                      head3] + v_specs,
            out_specs=head3),
        out_shape=jax.ShapeDtypeStruct((db, n_heads, head_dim), F32),
        compiler_params=_cparams(("arbitrary",)),
        name="sample_value",
    )(pt_flat, sel_flat, p_all, p_own, vn3, *([vt] * n_fetch))
    return out.reshape(db, da)


def _block_diag(w):
    n, wi, wo = w.shape
    eye = jnp.eye(n, dtype=w.dtype)
    return (eye[:, None, :, None] * w[:, :, None, :]).reshape(n * wi, n * wo)


def _pick_tile(n, cap):
    t = min(n, cap)
    while n % t:
        t //= 2
    return t


def kernel(x_prompt, x_sample, cache_k, cache_v, state_rglru_h, state_conv, page_table, rms_g,
           q_norm_g, k_norm_g, w_in, conv_w, conv_b, w_a, b_a, w_x, b_x, lru_lambda, w_out):
    depth = w_in.shape[0]
    batch, seq, d = x_prompt.shape
    db, dec_seq, _ = x_sample.shape
    _, n_pool, page, n_heads, head_dim = cache_k.shape
    da = n_heads * head_dim
    dr = state_rglru_h.shape[-1]
    n_pages = page_table.shape[1]
    past_len = n_pages * page
    assert dec_seq == 1, "sample group is one new token per sequence"
    assert w_in.shape[2] == 4 * da + 2 * dr and da % LANES == 0 and dr % LANES == 0

    tm = _pick_tile(seq, 512)
    tms = _pick_tile(db, 512)
    pos_p = jnp.arange(seq)
    pos_s = jnp.full((db,), past_len, jnp.int32)
    bd = _block_diag(jnp.ones((LANES // head_dim, head_dim, head_dim), BF16))
    cache_kt = jnp.transpose(cache_k, (0, 1, 3, 4, 2))
    cache_vt = jnp.transpose(cache_v, (0, 1, 3, 4, 2))
    pt_flat = page_table.reshape(-1).astype(jnp.int32)

    yp = x_prompt.reshape(batch * seq, d)
    ys = x_sample.reshape(db, d)
    outs = {n: [] for n in ("kp", "vp", "hp", "cp", "ks", "vs", "hs", "cs")}
    for l in range(depth):
        w_in_bf = w_in[l].astype(BF16)
        w_out_bf = w_out[l].astype(BF16)
        wax = jnp.concatenate([_block_diag(w_a[l]), _block_diag(w_x[l])], axis=1).astype(BF16)
        proj = functools.partial(_proj_in, rms_g=rms_g[l], gq=q_norm_g[l], gk=k_norm_g[l],
                                 w_bf=w_in_bf, bd=bd, da=da, dr=dr, head_dim=head_dim)
        rnn_w = (conv_w[l], conv_b[l], wax, b_a[l], b_x[l], lru_lambda[l])

        q, k, v, ga, xr, gr, ka, vb, km = proj(yp, pos_p, tm=tm, with_prompt_outs=True)
        q_s, k_s, v_s, ga_s, xr_s, gr_s = proj(ys, pos_s, tm=tms, with_prompt_outs=False)

        km = km[:, :tm // MOBA_BLOCK, :].reshape(batch, seq // MOBA_BLOCK, da)
        attn, s_all = _moba_prompt(q, ka, vb, km, q_s, cache_kt, pt_flat, l, batch=batch, seq=seq,
                                   n_pages=n_pages, head_dim=head_dim)
        rnn, h_last, cbuf = _rnn_prompt(xr, gr, *rnn_w, batch=batch, seq=seq, tr=tm)
        yp = _proj_out(yp, attn, ga, rnn, w_out_bf, tm=tm, tiles_per_seq=seq // tm)
        to_out = lambda t: t.reshape(batch, n_heads, head_dim, seq).transpose(0, 3, 1, 2)
        outs["kp"].append(to_out(k))
        outs["vp"].append(to_out(v))
        outs["hp"].append(h_last.reshape(batch, dr))
        outs["cp"].append(cbuf)

        q, k, v, ga, xr, gr = q_s, k_s, v_s, ga_s, xr_s, gr_s
        attn = _moba_sample(s_all, q, k, v, cache_vt, pt_flat, l, n_heads=n_heads, head_dim=head_dim)
        cs = state_conv[l].reshape(db, -1)
        rnn, h_last, cs_new = _rnn_sample(xr, gr, cs, state_rglru_h[l], *rnn_w)
        ys = _proj_out(ys, attn, ga, rnn, w_out_bf, tm=tms)
        outs["ks"].append(k.reshape(db, 1, n_heads, head_dim))
        outs["vs"].append(v.reshape(db, 1, n_heads, head_dim))
        outs["hs"].append(h_last)
        outs["cs"].append(cs_new.reshape(db, -1, dr))

    st = lambda n: jnp.stack(outs[n])
    return (yp.reshape(batch, seq, d), ys.reshape(db, 1, d), st("kp"), st("vp"), st("hp"),
            st("cp"), st("ks"), st("vs"), st("hs"), st("cs"))
```
